```python
import math
import jax, jax.numpy as jnp
from jax import lax
import numpy as np

D_MODEL = 4096
BATCH = 4
SEQ = 2048
DEPTH = 4
DEC_BATCH = 1
DEC_SEQ = 8192
PAST_LEN = 128

MIX_W = D_MODEL
FNET_W = MIX_W // 2
FNET_HEADS = 8
FNET_HD = FNET_W // FNET_HEADS
S5_W = MIX_W - FNET_W
S5_P = 16
S5_GROUPS = S5_W // S5_P
S5_N = 64
DT_MIN = 1e-3
DT_MAX = 1e-1
N_EXPERTS = 64
N_EXP_GROUPS = 8
EXP_PER_GROUP = N_EXPERTS // N_EXP_GROUPS
TOPK_GROUPS = 4
TOP_K = 8
D_EXPERT = D_MODEL // 8
D_SHARED = D_EXPERT
ROUTE_SCALE = 2.5
MOE_BLOCK = 128
ALPHA = (2 * DEPTH) ** 0.25
BETA = (8 * DEPTH) ** -0.25
EPS = 1e-5

kernel_name = "hymba_fnet_s5_deepseekmoe_deepnorm_encoder"


def _layernorm(x, g, b):
    xf = x.astype(jnp.float32)
    mu = jnp.mean(xf, axis=-1, keepdims=True)
    var = jnp.mean(jnp.square(xf - mu), axis=-1, keepdims=True)
    y = (xf - mu) * lax.rsqrt(var + EPS) * g.astype(jnp.float32) + b.astype(jnp.float32)
    return y.astype(x.dtype)


def _rmsnorm(x, g):
    xf = x.astype(jnp.float32)
    y = xf * lax.rsqrt(jnp.mean(jnp.square(xf), axis=-1, keepdims=True) + EPS) * g.astype(jnp.float32)
    return y.astype(x.dtype)


def _fnet_branch(u, w_map):
    b, s, _ = u.shape
    uh = u.astype(jnp.float32).reshape(b, s, FNET_HEADS, FNET_HD)
    mixed = jnp.fft.fft2(uh, axes=(1, 3), norm="ortho").real
    y = jnp.einsum("bshc,hcd->bshd", mixed.astype(u.dtype), w_map)
    return y.reshape(b, s, FNET_W)


def _ssm_combine(left, right):
    a_l, b_l = left
    a_r, b_r = right
    return a_l * a_r, a_r * b_l + b_r


def _s5_branch(u, a_re, a_im, log_dt, b_re, b_im, c_re, c_im, d_skip, glu_w, glu_b):
    b, s, _ = u.shape
    uf = u.astype(jnp.float32).reshape(b, s, S5_GROUPS, S5_P)
    uc = uf.astype(jnp.complex64)
    y = d_skip.astype(jnp.float32).reshape(S5_GROUPS, S5_P) * uf
    for direction in range(2):
        lam = lax.complex(a_re[direction].astype(jnp.float32), a_im[direction].astype(jnp.float32))
        dt = jnp.exp(log_dt[direction].astype(jnp.float32))[:, None]
        lam_bar = jnp.exp(lam * dt)
        b_mat = lax.complex(b_re[direction].astype(jnp.float32), b_im[direction].astype(jnp.float32))
        b_bar = ((lam_bar - 1.0) / lam)[..., None] * b_mat
        bu = jnp.einsum("bsgp,gnp->bsgn", uc, b_bar)
        decay = jnp.broadcast_to(lam_bar, bu.shape)
        _, states = lax.associative_scan(_ssm_combine, (decay, bu), axis=1, reverse=(direction == 1))
        c_mat = lax.complex(c_re[direction].astype(jnp.float32), c_im[direction].astype(jnp.float32))
        y = y + jnp.einsum("bsgn,gpn->bsgp", states, c_mat).real
    y = jax.nn.gelu(y.reshape(b, s, S5_W)).astype(u.dtype)
    gate = jax.nn.sigmoid((y @ glu_w + glu_b).astype(jnp.float32)).astype(u.dtype)
    return y * gate


def _swiglu(x, wg, wu, wd):
    return (jax.nn.silu(x @ wg) * (x @ wu)) @ wd


def _moe(x2, router_w, router_bias, w_gate, w_up, w_down, sh_gate, sh_up, sh_down):
    t, d = x2.shape
    scores = jax.nn.sigmoid((x2 @ router_w).astype(jnp.float32))
    sel = scores + router_bias.astype(jnp.float32)
    grp = sel.reshape(t, N_EXP_GROUPS, EXP_PER_GROUP)
    grp_score = jnp.sum(lax.top_k(grp, 2)[0], axis=-1)
    _, top_g = lax.top_k(grp_score, TOPK_GROUPS)
    gmask = jnp.sum(jax.nn.one_hot(top_g, N_EXP_GROUPS, dtype=jnp.int32), axis=1) > 0
    emask = jnp.repeat(gmask, EXP_PER_GROUP, axis=1)
    _, top_e = lax.top_k(jnp.where(emask, sel, -jnp.inf), TOP_K)
    w = jnp.take_along_axis(scores, top_e, axis=1)
    w = w / jnp.sum(w, axis=-1, keepdims=True) * ROUTE_SCALE

    n_assign = t * TOP_K
    e_flat = top_e.reshape(-1)
    tok_flat = jnp.repeat(jnp.arange(t, dtype=jnp.int32), TOP_K)
    order = jnp.argsort(e_flat)
    e_s = e_flat[order]
    tok_s = tok_flat[order]
    g_s = w.reshape(-1)[order]
    counts = jnp.bincount(e_flat, length=N_EXPERTS)
    starts = jnp.cumsum(counts) - counts
    padded = (counts + MOE_BLOCK - 1) // MOE_BLOCK * MOE_BLOCK
    pad_ends = jnp.cumsum(padded)
    pad_starts = pad_ends - padded
    dest = pad_starts[e_s] + (jnp.arange(n_assign, dtype=jnp.int32) - starts[e_s])
    n_blocks = -(-(n_assign + N_EXPERTS * (MOE_BLOCK - 1)) // MOE_BLOCK)
    n_slots = n_blocks * MOE_BLOCK
    tok_buf = jnp.full((n_slots,), t, dtype=jnp.int32).at[dest].set(tok_s)
    gate_buf = jnp.zeros((n_slots,), jnp.float32).at[dest].set(g_s)
    block_start = jnp.arange(n_blocks, dtype=jnp.int32) * MOE_BLOCK
    block_expert = jnp.minimum(jnp.searchsorted(pad_ends, block_start, side="right"), N_EXPERTS - 1)
    x_pad = jnp.concatenate([x2, jnp.zeros((1, d), x2.dtype)], axis=0)
    xb = x_pad[tok_buf].reshape(n_blocks, MOE_BLOCK, d)

    def expert_block(args):
        xblk, e = args
        return _swiglu(xblk, w_gate[e], w_up[e], w_down[e])

    yb = lax.map(expert_block, (xb, block_expert)).reshape(n_slots, d)
    routed = jnp.zeros((t + 1, d), x2.dtype).at[tok_buf].add(yb * gate_buf[:, None].astype(yb.dtype))[:t]
    return routed + _swiglu(x2, sh_gate, sh_up, sh_down)


def _trunk(x, w_in, fnet_w, s5_a_re, s5_a_im, s5_log_dt, s5_b_re, s5_b_im, s5_c_re, s5_c_im,
           s5_d, glu_w, glu_b, gn_fnet, gn_s5, w_out, ln1_g, ln1_b, router_w, router_bias,
           exp_w_gate, exp_w_up, exp_w_down, sh_w_gate, sh_w_up, sh_w_down, ln2_g, ln2_b):
    b, s, d = x.shape
    for l in range(DEPTH):
        h = x @ w_in[l]
        f = _fnet_branch(h[..., :FNET_W], fnet_w[l])
        sm = _s5_branch(h[..., FNET_W:], s5_a_re[l], s5_a_im[l], s5_log_dt[l], s5_b_re[l], s5_b_im[l],
                        s5_c_re[l], s5_c_im[l], s5_d[l], glu_w[l], glu_b[l])
        mix = jnp.concatenate([_rmsnorm(f, gn_fnet[l]), _rmsnorm(sm, gn_s5[l])], axis=-1)
        x = _layernorm(ALPHA * x + mix @ w_out[l], ln1_g[l], ln1_b[l])
        moe = _moe(x.reshape(b * s, d), router_w[l], router_bias[l], exp_w_gate[l], exp_w_up[l],
                   exp_w_down[l], sh_w_gate[l], sh_w_up[l], sh_w_down[l]).reshape(b, s, d)
        x = _layernorm(ALPHA * x + moe, ln2_g[l], ln2_b[l])
    return x


def setup_inputs(seed: int = 0) -> dict:
    key = jax.random.key(seed)
    ks = jax.random.split(key, 32)
    f32 = jnp.float32

    def nrm(k, shape, scale):
        return jax.random.normal(k, shape, f32) * scale

    x_prompt = nrm(ks[0], (BATCH, SEQ, D_MODEL), 1.0)
    x_sample = nrm(ks[1], (DEC_BATCH, DEC_SEQ, D_MODEL), 1.0)
    w_in = nrm(ks[2], (DEPTH, D_MODEL, MIX_W), D_MODEL ** -0.5)
    fnet_w = nrm(ks[3], (DEPTH, FNET_HEADS, FNET_HD, FNET_HD), FNET_HD ** -0.5)
    s5_a_re = -0.5 + nrm(ks[4], (DEPTH, 2, S5_GROUPS, S5_N), 0.01)
    s5_a_im = math.pi * jnp.arange(S5_N, dtype=f32) + nrm(ks[5], (DEPTH, 2, S5_GROUPS, S5_N), 0.01)
    s5_log_dt = jax.random.uniform(ks[6], (DEPTH, 2, S5_GROUPS), f32, math.log(DT_MIN), math.log(DT_MAX))
    s5_b_re = nrm(ks[7], (DEPTH, 2, S5_GROUPS, S5_N, S5_P), (2 * S5_P) ** -0.5)
    s5_b_im = nrm(ks[8], (DEPTH, 2, S5_GROUPS, S5_N, S5_P), (2 * S5_P) ** -0.5)
    s5_c_re = nrm(ks[9], (DEPTH, 2, S5_GROUPS, S5_P, S5_N), (2 * S5_N) ** -0.5)
    s5_c_im = nrm(ks[10], (DEPTH, 2, S5_GROUPS, S5_P, S5_N), (2 * S5_N) ** -0.5)
    s5_d = nrm(ks[11], (DEPTH, S5_W), 1.0)
    glu_w = nrm(ks[12], (DEPTH, S5_W, S5_W), S5_W ** -0.5)
    glu_b = nrm(ks[13], (DEPTH, S5_W), 0.01)
    gn_fnet = 1.0 + nrm(ks[14], (DEPTH, FNET_W), 0.01)
    gn_s5 = 1.0 + nrm(ks[15], (DEPTH, S5_W), 0.01)
    w_out = nrm(ks[16], (DEPTH, MIX_W, D_MODEL), BETA * MIX_W ** -0.5)
    ln1_g = 1.0 + nrm(ks[17], (DEPTH, D_MODEL), 0.01)
    ln1_b = nrm(ks[18], (DEPTH, D_MODEL), 0.01)
    router_w = nrm(ks[19], (DEPTH, D_MODEL, N_EXPERTS), D_MODEL ** -0.5)
    router_bias = nrm(ks[20], (DEPTH, N_EXPERTS), 0.01)
    exp_w_gate = nrm(ks[21], (DEPTH, N_EXPERTS, D_MODEL, D_EXPERT), D_MODEL ** -0.5)
    exp_w_up = nrm(ks[22], (DEPTH, N_EXPERTS, D_MODEL, D_EXPERT), D_MODEL ** -0.5)
    exp_w_down = nrm(ks[23], (DEPTH, N_EXPERTS, D_EXPERT, D_MODEL), BETA * D_EXPERT ** -0.5)
    sh_w_gate = nrm(ks[24], (DEPTH, D_MODEL, D_SHARED), D_MODEL ** -0.5)
    sh_w_up = nrm(ks[25], (DEPTH, D_MODEL, D_SHARED), D_MODEL ** -0.5)
    sh_w_down = nrm(ks[26], (DEPTH, D_SHARED, D_MODEL), BETA * D_SHARED ** -0.5)
    ln2_g = 1.0 + nrm(ks[27], (DEPTH, D_MODEL), 0.01)
    ln2_b = nrm(ks[28], (DEPTH, D_MODEL), 0.01)
    return {"x_prompt": x_prompt, "x_sample": x_sample, "w_in": w_in, "fnet_w": fnet_w,
            "s5_a_re": s5_a_re, "s5_a_im": s5_a_im, "s5_log_dt": s5_log_dt,
            "s5_b_re": s5_b_re, "s5_b_im": s5_b_im, "s5_c_re": s5_c_re, "s5_c_im": s5_c_im,
            "s5_d": s5_d, "glu_w": glu_w, "glu_b": glu_b, "gn_fnet": gn_fnet, "gn_s5": gn_s5,
            "w_out": w_out, "ln1_g": ln1_g, "ln1_b": ln1_b, "router_w": router_w,
            "router_bias": router_bias, "exp_w_gate": exp_w_gate, "exp_w_up": exp_w_up,
            "exp_w_down": exp_w_down, "sh_w_gate": sh_w_gate, "sh_w_up": sh_w_up,
            "sh_w_down": sh_w_down, "ln2_g": ln2_g, "ln2_b": ln2_b}


def reference(x_prompt, x_sample, w_in, fnet_w, s5_a_re, s5_a_im, s5_log_dt, s5_b_re, s5_b_im,
              s5_c_re, s5_c_im, s5_d, glu_w, glu_b, gn_fnet, gn_s5, w_out, ln1_g, ln1_b,
              router_w, router_bias, exp_w_gate, exp_w_up, exp_w_down, sh_w_gate, sh_w_up,
              sh_w_down, ln2_g, ln2_b):
    params = (w_in, fnet_w, s5_a_re, s5_a_im, s5_log_dt, s5_b_re, s5_b_im, s5_c_re, s5_c_im,
              s5_d, glu_w, glu_b, gn_fnet, gn_s5, w_out, ln1_g, ln1_b, router_w, router_bias,
              exp_w_gate, exp_w_up, exp_w_down, sh_w_gate, sh_w_up, sh_w_down, ln2_g, ln2_b)
    y_prompt = _trunk(x_prompt, *params)
    y_sample = _trunk(x_sample, *params)
    return (y_prompt, y_sample)
```

```python
import functools
import math

import numpy as np
import jax
import jax.numpy as jnp
from jax import lax
from jax.experimental import pallas as pl
from jax.experimental.pallas import tpu as pltpu

F32 = jnp.float32
BF16 = jnp.bfloat16
I32 = jnp.int32
U32 = jnp.uint32

DEPTH = 4
FNET_HEADS = 8
S5_P = 16
S5_N = 64
N_EXPERTS = 64
N_EXP_GROUPS = 8
EXP_PER_GROUP = N_EXPERTS // N_EXP_GROUPS
TOPK_GROUPS = 4
TOP_K = 8
ROUTE_SCALE = 2.5
ALPHA = (2 * DEPTH) ** 0.25
EPS = 1e-5

S5_L = 16
S5_PAIR = 2
MOE_ROWS = 256
V7X_VMEM_BYTES = 64 * 1024 * 1024
VMEM_CAP = V7X_VMEM_BYTES - 8 * 1024 * 1024
LANES = 128


def _cparams(sem, vmem_mb):
    return pltpu.CompilerParams(dimension_semantics=sem,
                                vmem_limit_bytes=min(int(vmem_mb * 1024 * 1024), VMEM_CAP))


def _mm_body(x_ref, w_ref, o_ref):
    o_ref[...] = jnp.dot(x_ref[...], w_ref[...], preferred_element_type=F32).astype(o_ref.dtype)


def _mm(x, w, tm, tn, out_dtype, name):
    m, k = x.shape
    n = w.shape[1]
    return pl.pallas_call(
        _mm_body, grid=(m // tm, n // tn),
        in_specs=[pl.BlockSpec((tm, k), lambda i, j: (i, 0)), pl.BlockSpec((k, tn), lambda i, j: (0, j))],
        out_specs=pl.BlockSpec((tm, tn), lambda i, j: (i, j)),
        out_shape=jax.ShapeDtypeStruct((m, n), out_dtype),
        compiler_params=_cparams(("parallel", "parallel"), 48), name=name)(x, w)


def _mm_res_body(x_ref, w_ref, r_ref, o_ref):
    acc = jnp.dot(x_ref[...], w_ref[...], preferred_element_type=F32)
    o_ref[...] = ALPHA * r_ref[...] + acc


def _mm_res(x, w, res, tm, tn, name):
    m, k = x.shape
    n = w.shape[1]
    return pl.pallas_call(
        _mm_res_body, grid=(m // tm, n // tn),
        in_specs=[pl.BlockSpec((tm, k), lambda i, j: (i, 0)), pl.BlockSpec((k, tn), lambda i, j: (0, j)),
                  pl.BlockSpec((tm, tn), lambda i, j: (i, j))],
        out_specs=pl.BlockSpec((tm, tn), lambda i, j: (i, j)),
        out_shape=jax.ShapeDtypeStruct((m, n), F32),
        compiler_params=_cparams(("parallel", "parallel"), 48), name=name)(x, w, res)


def _glu_body(y_ref, w_ref, b_ref, yt_ref, o_ref):
    g = jnp.dot(y_ref[...], w_ref[...], preferred_element_type=F32) + b_ref[...]
    o_ref[...] = (yt_ref[...].astype(F32) * jax.nn.sigmoid(g)).astype(o_ref.dtype)


def _glu(y, w, b, tm, tn):
    m, k = y.shape
    n = w.shape[1]
    return pl.pallas_call(
        _glu_body, grid=(m // tm, n // tn),
        in_specs=[pl.BlockSpec((tm, k), lambda i, j: (i, 0)), pl.BlockSpec((k, tn), lambda i, j: (0, j)),
                  pl.BlockSpec((1, tn), lambda i, j: (0, j)), pl.BlockSpec((tm, tn), lambda i, j: (i, j))],
        out_specs=pl.BlockSpec((tm, tn), lambda i, j: (i, j)),
        out_shape=jax.ShapeDtypeStruct((m, n), BF16),
        compiler_params=_cparams(("parallel", "parallel"), 32), name="s5_glu")(y, w, b, y)


def _dft_consts(n1, n2, hd):
    n = n1 * n2
    k = np.arange(n1)
    ang1 = 2 * np.pi * np.outer(k, k) / n1
    c1, s1 = np.cos(ang1), np.sin(ang1)
    w1 = np.block([[c1, s1], [-s1, c1]])
    ang_t = 2 * np.pi * np.outer(np.arange(n2), np.arange(n1)) / n
    twr = np.broadcast_to(np.cos(ang_t)[:, :, None], (n2, n1, LANES))
    twi = np.broadcast_to(-np.sin(ang_t)[:, :, None], (n2, n1, LANES))
    k2 = np.arange(n2)
    ang2 = 2 * np.pi * np.outer(k2, k2) / n2
    w2 = np.concatenate([np.cos(ang2), np.sin(ang2)], axis=1) / math.sqrt(n)
    return (jnp.asarray(w1, BF16), jnp.asarray(np.ascontiguousarray(twr), F32),
            jnp.asarray(np.ascontiguousarray(twi), F32), jnp.asarray(w2, BF16))


def _chan_const(hd):
    c = np.arange(hd)
    ang = 2 * np.pi * np.outer(c, c) / hd
    return jnp.asarray(np.concatenate([np.cos(ang), -np.sin(ang)], axis=1) / math.sqrt(hd), BF16)


def _chan_body(x_ref, w_ref, zr_ref, zi_ref, *, hd):
    r = jnp.dot(x_ref[...], w_ref[...], preferred_element_type=F32)
    zr_ref[...] = r[:, :hd].astype(BF16)
    zi_ref[...] = r[:, hd:].astype(BF16)


def _fnet_chan(h, wc, heads, hd, tm):
    t = h.shape[0]
    shp = jax.ShapeDtypeStruct((t, heads * hd), BF16)
    return pl.pallas_call(
        functools.partial(_chan_body, hd=hd), grid=(t // tm, heads),
        in_specs=[pl.BlockSpec((tm, hd), lambda i, j: (i, j)), pl.BlockSpec((hd, 2 * hd), lambda i, j: (0, 0))],
        out_specs=[pl.BlockSpec((tm, hd), lambda i, j: (i, j))] * 2,
        out_shape=[shp, shp],
        compiler_params=_cparams(("parallel", "parallel"), 32), name="fnet_chan")(h, wc)


def _stage1_body(zr_ref, zi_ref, twr_ref, twi_ref, w_ref, yr_ref, yi_ref, *, n2t, n1, reps):
    w = w_ref[...]
    for j in range(n2t):
        z = jnp.concatenate([zr_ref[j], zi_ref[j]], axis=0)
        y = jnp.dot(w, z, preferred_element_type=F32)
        yr, yi = y[:n1], y[n1:]
        twr = jnp.concatenate([twr_ref[j]] * reps, axis=1)
        twi = jnp.concatenate([twi_ref[j]] * reps, axis=1)
        yr_ref[j] = (yr * twr - yi * twi).astype(BF16)
        yi_ref[j] = (yr * twi + yi * twr).astype(BF16)


def _fnet_stage1(zr, zi, w1, twr, twi, n2t, ct):
    nseq, n2, n1, c = zr.shape
    blk = pl.BlockSpec((None, n2t, n1, ct), lambda s, a, b: (s, a, 0, b))
    twb = pl.BlockSpec((n2t, n1, LANES), lambda s, a, b: (a, 0, 0))
    shp = jax.ShapeDtypeStruct(zr.shape, BF16)
    return pl.pallas_call(
        functools.partial(_stage1_body, n2t=n2t, n1=n1, reps=ct // LANES),
        grid=(nseq, n2 // n2t, c // ct),
        in_specs=[blk, blk, twb, twb, pl.BlockSpec((2 * n1, 2 * n1), lambda s, a, b: (0, 0))],
        out_specs=[blk, blk], out_shape=[shp, shp],
        compiler_params=_cparams(("parallel", "parallel", "parallel"), 32),
        name=f"fnet_stage1_n{n1}")(zr, zi, twr, twi, w1)


def _stage2_body(yr_ref, yi_ref, w_ref, wm_ref, f_ref, *, k1t):
    w = w_ref[...]
    wm = wm_ref[0]
    for j in range(k1t):
        yy = jnp.concatenate([yr_ref[j], yi_ref[j]], axis=0)
        r = jnp.dot(w, yy, preferred_element_type=F32)
        f_ref[j] = jnp.dot(r.astype(BF16), wm, preferred_element_type=F32).astype(BF16)


def _fnet_stage2(yr, yi, w2, wmap, k1t, hd):
    nseq, n1, n2, c = yr.shape
    blk = pl.BlockSpec((None, k1t, n2, hd), lambda s, a, h: (s, a, 0, h))
    return pl.pallas_call(
        functools.partial(_stage2_body, k1t=k1t), grid=(nseq, n1 // k1t, c // hd),
        in_specs=[blk, blk, pl.BlockSpec((n2, 2 * n2), lambda s, a, h: (0, 0)),
                  pl.BlockSpec((1, hd, hd), lambda s, a, h: (h, 0, 0))],
        out_specs=blk, out_shape=jax.ShapeDtypeStruct(yr.shape, BF16),
        compiler_params=_cparams(("parallel", "parallel", "parallel"), 32),
        name=f"fnet_stage2_n{n1}")(yr, yi, w2, wmap)


def _fnet_branch(h, wmap_b, seqs, consts, heads, hd):
    c = heads * hd
    zr, zi = _fnet_chan(h, consts["chan"], heads, hd, tm=2048)
    outs = []
    for (row0, nseq, n1, n2) in seqs:
        rows = nseq * n1 * n2
        w1, twr, twi, w2 = consts[(n1, n2)]

        def to_stage1(z):
            return z[row0:row0 + rows].reshape(nseq, n1, n2, c).transpose(0, 2, 1, 3)

        yr, yi = _fnet_stage1(to_stage1(zr), to_stage1(zi), w1, twr, twi, n2t=8, ct=min(c, 1024))
        yr = yr.transpose(0, 2, 1, 3)
        yi = yi.transpose(0, 2, 1, 3)
        f = _fnet_stage2(yr, yi, w2, wmap_b, k1t=8, hd=hd)
        outs.append(f.transpose(0, 2, 1, 3).reshape(rows, c))
    return jnp.concatenate(outs, axis=0)


def _s5_prep_body(are_ref, aim_ref, ldt_ref, brt_ref, bit_ref, cr_ref, ci_ref, crt_ref, cit_ref, dd_ref,
                  m_ref, p_ref, q_ref, lam_ref):
    L, P = S5_L, S5_P
    R = L * P
    lane = lax.broadcasted_iota(I32, (R, LANES), 1)
    member = lane // S5_N
    slot = lane // P
    row = lax.broadcasted_iota(I32, (L, LANES), 0).astype(F32)

    def powers(ar, ai, dt, tau):
        e = jnp.exp(tau * (ar * dt))
        ang = tau * (ai * dt)
        return e * jnp.cos(ang), e * jnp.sin(ang)

    def expand(x):
        return jnp.concatenate([jnp.broadcast_to(x[t:t + 1], (P, LANES)) for t in range(L)], axis=0)

    def tile(x):
        return jnp.concatenate([x] * L, axis=0)

    def cmul(ar_, ai_, br_, bi_):
        return ar_ * br_ - ai_ * bi_, ar_ * bi_ + ai_ * br_

    gs, qs, kts, lams = [], [], [], []
    for d in (0, 1):
        ar, ai = are_ref[d], aim_ref[d]
        dt = jnp.exp(ldt_ref[d])
        one = jnp.ones((1, LANES), F32)
        lr, li = powers(ar, ai, dt, one)
        den = ar * ar + ai * ai
        xr, xi = lr - 1.0, li
        cfr = (xr * ar + xi * ai) / den
        cfi = (xi * ar - xr * ai) / den
        bbr, bbi = cmul(cfr, cfi, brt_ref[d], bit_ref[d])
        tau_g = (L - 1.0 - row) if d == 0 else row
        tau_q = (row + 1.0) if d == 0 else (L - row)
        pr, pi = powers(ar, ai, dt, tau_g)
        gr, gi = cmul(expand(pr), expand(pi), tile(bbr), tile(bbi))
        qr, qi = powers(ar, ai, dt, tau_q)
        er, ei = cmul(tile(cr_ref[d]), tile(ci_ref[d]), expand(qr), expand(qi))
        gs.append((gr, gi))
        qs.append((er, -ei))
        lams.extend(powers(ar, ai, dt, float(L) * one))
        rhs = jnp.concatenate([crt_ref[d], cit_ref[d]], axis=0)
        per_member = []
        for g in range(S5_PAIR):
            msk = member == g
            lhs = jnp.concatenate([jnp.where(msk, gr, 0.0), jnp.where(msk, -gi, 0.0)], axis=1)
            per_member.append(jnp.dot(lhs, rhs, preferred_element_type=F32, precision=lax.Precision.HIGHEST))
        kts.append(per_member)

    m_ref[...] = jnp.zeros(m_ref.shape, m_ref.dtype)
    for g in range(S5_PAIR):
        ktf, ktb = kts[0][g], kts[1][g]
        v = jnp.concatenate([ktf[:R - P], ktf[R - P:] + ktb[:P] + dd_ref[g], ktb[P:]], axis=0)
        for hh in range(R // LANES):
            acc = jnp.zeros((R, LANES), F32)
            for il in range(LANES // P):
                i = hh * (LANES // P) + il
                acc = jnp.where(slot == il, v[(L - 1 - i) * P:(L - 1 - i) * P + R], acc)
            m_ref[g * R:(g + 1) * R, g * R + hh * LANES:g * R + (hh + 1) * LANES] = acc.astype(m_ref.dtype)
        msk = member == g
        comps_p = [gs[0][0], gs[0][1], gs[1][0], gs[1][1]]
        comps_q = [qs[0][0], qs[0][1], qs[1][0], qs[1][1]]
        for c in range(4):
            p_ref[g * R:(g + 1) * R, c * LANES:(c + 1) * LANES] = jnp.where(msk, comps_p[c], 0.0).astype(p_ref.dtype)
            q_ref[g * R:(g + 1) * R, c * LANES:(c + 1) * LANES] = jnp.where(msk, comps_q[c], 0.0).astype(q_ref.dtype)
    lam_ref[...] = jnp.concatenate(lams, axis=1)


def _s5_prep(a_re, a_im, log_dt, b_re, b_im, c_re, c_im, d_skip):
    g, n, p = b_re.shape[1], b_re.shape[2], b_re.shape[3]
    sg = g // S5_PAIR
    assert n == S5_N and p == S5_P and S5_PAIR * n == LANES
    lanes4 = lambda x: x.reshape(2, sg, 1, LANES).astype(F32)
    are, aim = lanes4(a_re), lanes4(a_im)
    ldt = lanes4(jnp.repeat(log_dt, n, axis=1))
    bt = lambda x: x.reshape(2, sg, S5_PAIR, n, p).transpose(0, 1, 4, 2, 3).reshape(2, sg, p, LANES).astype(F32)
    cc = lambda x: x.reshape(2, sg, S5_PAIR, p, n).transpose(0, 1, 3, 2, 4).reshape(2, sg, p, LANES).astype(F32)
    ct = lambda x: jnp.tile(x.reshape(2, sg, S5_PAIR, p, n).transpose(0, 1, 2, 4, 3).reshape(2, sg, LANES, p),
                            (1, 1, 1, LANES // p)).astype(F32)
    eye = jnp.eye(p, dtype=F32)
    dd = d_skip.reshape(sg, S5_PAIR, 1, p).astype(F32) * eye[None, None]
    dd = jnp.tile(dd, (1, 1, 1, LANES // p))
    r2 = S5_PAIR * S5_L * p
    v1 = pl.BlockSpec((2, None, 1, LANES), lambda s: (0, s, 0, 0))
    vp = pl.BlockSpec((2, None, p, LANES), lambda s: (0, s, 0, 0))
    vt = pl.BlockSpec((2, None, LANES, LANES), lambda s: (0, s, 0, 0))
    mat = pl.BlockSpec((None, r2, r2), lambda s: (s, 0, 0))
    mshape = jax.ShapeDtypeStruct((sg, r2, r2), BF16)
    return pl.pallas_call(
        _s5_prep_body, grid=(sg,),
        in_specs=[v1, v1, v1, vp, vp, vp, vp, vt, vt, pl.BlockSpec((None, S5_PAIR, p, LANES), lambda s: (s, 0, 0, 0))],
        out_specs=[mat, mat, mat, pl.BlockSpec((None, 1, 4 * LANES), lambda s: (s, 0, 0))],
        out_shape=[mshape, mshape, mshape, jax.ShapeDtypeStruct((sg, 1, 4 * LANES), F32)],
        compiler_params=_cparams(("parallel",), 32), name="s5_prep")(
            are, aim, ldt, bt(b_re), bt(b_im), cc(c_re), cc(c_im), ct(c_re), ct(c_im), dd)


def _s5_statein(u, pm, nct):
    sg, nc, r2 = u.shape
    return pl.pallas_call(
        _mm_body, grid=(sg, nc // nct),
        in_specs=[pl.BlockSpec((None, nct, r2), lambda s, c: (s, c, 0)),
                  pl.BlockSpec((None, r2, pm.shape[2]), lambda s, c: (s, 0, 0))],
        out_specs=pl.BlockSpec((None, nct, pm.shape[2]), lambda s, c: (s, c, 0)),
        out_shape=jax.ShapeDtypeStruct((sg, nc, pm.shape[2]), F32),
        compiler_params=_cparams(("parallel", "parallel"), 32), name="s5_statein")(u, pm)


def _s5_scan_body(fk_ref, bk_ref, rs_ref, pf_ref, pb_ref, lam_ref, sf_ref, sb_ref, st_ref, *, nct):
    k = pl.program_id(0)

    @pl.when(rs_ref[k] == 1)
    def _():
        st_ref[...] = jnp.zeros(st_ref.shape, F32)

    lam = lam_ref[...]
    lfr, lfi = lam[:, 0:LANES], lam[:, LANES:2 * LANES]
    lbr, lbi = lam[:, 2 * LANES:3 * LANES], lam[:, 3 * LANES:]

    def step(i, carry):
        fr, fi, br, bi = carry
        cb = nct - 1 - i
        sf_ref[i] = jnp.concatenate([fr, fi], axis=1).astype(BF16)
        sb_ref[cb] = jnp.concatenate([br, bi], axis=1).astype(BF16)
        pf = pf_ref[i]
        pb = pb_ref[cb]
        fr2 = lfr * fr - lfi * fi + pf[:, :LANES]
        fi2 = lfr * fi + lfi * fr + pf[:, LANES:]
        br2 = lbr * br - lbi * bi + pb[:, :LANES]
        bi2 = lbr * bi + lbi * br + pb[:, LANES:]
        return fr2, fi2, br2, bi2

    out = lax.fori_loop(0, nct, step, (st_ref[0], st_ref[1], st_ref[2], st_ref[3]))
    for j in range(4):
        st_ref[j] = out[j]


def _s5_scan(pin, lam, seq_chunks, nct):
    nc, sg, _ = pin.shape
    fk, bk, rs = [], [], []
    base = 0
    for n_chunks in seq_chunks:
        nb = n_chunks // nct
        for j in range(nb):
            fk.append(base + j)
            bk.append(base + nb - 1 - j)
            rs.append(1 if j == 0 else 0)
        base += nb
    nsteps = len(fk)
    to = lambda v: jnp.asarray(np.asarray(v, np.int32))
    blk_f = pl.BlockSpec((nct, sg, 2 * LANES), lambda k, fk_, bk_, rs_: (fk_[k], 0, 0))
    blk_b = pl.BlockSpec((nct, sg, 2 * LANES), lambda k, fk_, bk_, rs_: (bk_[k], 0, 1))
    out_f = pl.BlockSpec((nct, sg, 2 * LANES), lambda k, fk_, bk_, rs_: (fk_[k], 0, 0))
    out_b = pl.BlockSpec((nct, sg, 2 * LANES), lambda k, fk_, bk_, rs_: (bk_[k], 0, 0))
    shp = jax.ShapeDtypeStruct((nc, sg, 2 * LANES), BF16)
    return pl.pallas_call(
        functools.partial(_s5_scan_body, nct=nct),
        grid_spec=pltpu.PrefetchScalarGridSpec(
            num_scalar_prefetch=3, grid=(nsteps,),
            in_specs=[blk_f, blk_b, pl.BlockSpec((sg, 4 * LANES), lambda k, *_: (0, 0))],
            out_specs=[out_f, out_b],
            scratch_shapes=[pltpu.VMEM((4, sg, LANES), F32)]),
        out_shape=[shp, shp],
        compiler_params=_cparams(("arbitrary",), 40), name="s5_scan")(to(fk), to(bk), to(rs), pin, pin, lam)


def _s5_out_body(u_ref, m_ref, q_ref, sf_ref, sb_ref, y_ref):
    y = jnp.dot(u_ref[...], m_ref[...], preferred_element_type=F32)
    s = jnp.concatenate([sf_ref[...], sb_ref[...]], axis=1)
    y = y + lax.dot_general(s, q_ref[...], (((1,), (1,)), ((), ())), preferred_element_type=F32)
    y_ref[...] = jax.nn.gelu(y).astype(y_ref.dtype)


def _s5_out(u, mm, qt, sf, sb, nct):
    sg, nc, r2 = u.shape
    ublk = pl.BlockSpec((None, nct, r2), lambda s, c: (s, c, 0))
    sblk = pl.BlockSpec((None, nct, 2 * LANES), lambda s, c: (s, c, 0))
    mblk = pl.BlockSpec((None, r2, r2), lambda s, c: (s, 0, 0))
    return pl.pallas_call(
        _s5_out_body, grid=(sg, nc // nct),
        in_specs=[ublk, mblk, mblk, sblk, sblk], out_specs=ublk,
        out_shape=jax.ShapeDtypeStruct(u.shape, BF16),
        compiler_params=_cparams(("parallel", "parallel"), 32), name="s5_out")(u, mm, qt, sf, sb)


def _s5_branch(h, col0, ops, seq_lens):
    mm, pm, qt, lam = ops
    t = h.shape[0]
    sg = mm.shape[0]
    w = sg * S5_PAIR * S5_P
    nc = t // S5_L
    u = h[:, col0:col0 + w].reshape(nc, S5_L, sg, S5_PAIR, S5_P)
    u = u.transpose(2, 0, 3, 1, 4).reshape(sg, nc, S5_PAIR * S5_L * S5_P)
    pin = _s5_statein(u, pm, nct=nc // 2).transpose(1, 0, 2)
    sf, sb = _s5_scan(pin, lam.reshape(sg, 4 * LANES), [s // S5_L for s in seq_lens], nct=64)
    yg = _s5_out(u, mm, qt, sf.transpose(1, 0, 2), sb.transpose(1, 0, 2), nct=nc // 2)
    yg = yg.reshape(sg, nc, S5_PAIR, S5_L, S5_P).transpose(1, 3, 0, 2, 4)
    return yg.reshape(t, w)


def _mixnorm_body(f_ref, s_ref, gf_ref, gs_ref, o_ref, *, w):
    def rn(x, g):
        xf = x.astype(F32)
        return (xf * lax.rsqrt(jnp.mean(xf * xf, axis=-1, keepdims=True) + EPS) * g).astype(o_ref.dtype)
    o_ref[:, :w] = rn(f_ref[...], gf_ref[...])
    o_ref[:, w:] = rn(s_ref[...], gs_ref[...])


def _mixnorm(f, s, gf, gs, tm):
    t, w = f.shape
    return pl.pallas_call(
        functools.partial(_mixnorm_body, w=w), grid=(t // tm,),
        in_specs=[pl.BlockSpec((tm, w), lambda i: (i, 0)), pl.BlockSpec((tm, s.shape[1]), lambda i: (i, 0)),
                  pl.BlockSpec((1, w), lambda i: (0, 0)), pl.BlockSpec((1, s.shape[1]), lambda i: (0, 0))],
        out_specs=pl.BlockSpec((tm, w + s.shape[1]), lambda i: (i, 0)),
        out_shape=jax.ShapeDtypeStruct((t, w + s.shape[1]), BF16),
        compiler_params=_cparams(("parallel",), 32), name="mix_norm")(f, s, gf, gs)


def _layernorm(y, g, b):
    mu = jnp.mean(y, axis=-1, keepdims=True)
    yc = y - mu
    var = jnp.mean(yc * yc, axis=-1, keepdims=True)
    return yc * lax.rsqrt(var + EPS) * g + b


def _pack_halves(ob):
    hcols = ob.shape[1] // 2
    hi = lax.bitcast_convert_type(ob[:, :hcols].astype(F32), U32)
    lo = lax.bitcast_convert_type(ob[:, hcols:].astype(F32), U32)
    return hi | (lo >> 16)


def _unpack_halves(words):
    hi = lax.bitcast_convert_type(words & jnp.uint32(0xFFFF0000), F32)
    lo = lax.bitcast_convert_type(words << 16, F32)
    return hi, lo


def _ln1_body(y_ref, g_ref, b_ref, x_ref, xb_ref, xw_ref):
    o = _layernorm(y_ref[...], g_ref[...], b_ref[...])
    x_ref[...] = o
    ob = o.astype(BF16)
    xb_ref[...] = ob
    xw_ref[...] = _pack_halves(ob)


def _ln1(y, g, b, tm):
    t, d = y.shape
    row = pl.BlockSpec((tm, d), lambda i: (i, 0))
    vec = pl.BlockSpec((1, d), lambda i: (0, 0))
    return pl.pallas_call(
        _ln1_body, grid=(t // tm,), in_specs=[row, vec, vec],
        out_specs=[row, row, pl.BlockSpec((tm, d // 2), lambda i: (i, 0))],
        out_shape=[jax.ShapeDtypeStruct((t, d), F32), jax.ShapeDtypeStruct((t, d), BF16),
                   jax.ShapeDtypeStruct((t, d // 2), U32)],
        compiler_params=_cparams(("parallel",), 48), name="ln1")(y, g, b)


def _route_body(x_ref, wt_ref, bias_ref, tri_ref, e_ref, w_ref, r_ref, cnt_ref, crow_ref, carry_ref, rowc_ref):
    i = pl.program_id(0)

    @pl.when(i == 0)
    def _():
        carry_ref[...] = jnp.zeros(carry_ref.shape, F32)
        rowc_ref[...] = jnp.zeros(rowc_ref.shape, F32)

    ne = wt_ref.shape[0]
    tm = x_ref.shape[0]
    x = x_ref[...]
    xh = x.astype(BF16)
    xl = (x - xh.astype(F32)).astype(BF16)
    wt = wt_ref[...]
    wh = wt.astype(BF16)
    wl = (wt - wh.astype(F32)).astype(BF16)
    dn = (((1,), (1,)), ((), ()))
    dg = lambda a, b: lax.dot_general(a, b, dn, preferred_element_type=F32)
    logits = dg(wh, xh) + (dg(wh, xl) + dg(wl, xh))
    scores = jax.nn.sigmoid(logits)
    sel = scores + bias_ref[...]
    neg = jnp.float32(-jnp.inf)

    sub = lax.broadcasted_iota(I32, (EXP_PER_GROUP, tm), 0)
    gscore = []
    for g in range(N_EXP_GROUPS):
        blk = sel[g * EXP_PER_GROUP:(g + 1) * EXP_PER_GROUP]
        m1 = jnp.max(blk, axis=0, keepdims=True)
        i1 = jnp.min(jnp.where(blk == m1, sub, EXP_PER_GROUP), axis=0, keepdims=True)
        m2 = jnp.max(jnp.where(sub == i1, neg, blk), axis=0, keepdims=True)
        gscore.append(m1 + m2)
    keep = []
    for g in range(N_EXP_GROUPS):
        beat = jnp.zeros((1, tm), I32)
        for g2 in range(N_EXP_GROUPS):
            if g2 == g:
                continue
            wins = (gscore[g2] > gscore[g]) | ((gscore[g2] == gscore[g]) & (g2 < g))
            beat = beat + wins.astype(I32)
        keep.append(beat < TOPK_GROUPS)
    cand = jnp.concatenate(
        [jnp.where(keep[g], sel[g * EXP_PER_GROUP:(g + 1) * EXP_PER_GROUP], neg) for g in range(N_EXP_GROUPS)],
        axis=0)
    eid = lax.broadcasted_iota(I32, (ne, tm), 0)
    onehot = jnp.zeros((ne, tm), F32)
    idxs, wts = [], []
    for _ in range(TOP_K):
        m = jnp.max(cand, axis=0, keepdims=True)
        idx = jnp.min(jnp.where(cand == m, eid, ne), axis=0, keepdims=True)
        hit = eid == idx
        idxs.append(idx)
        wts.append(jnp.sum(jnp.where(hit, scores, 0.0), axis=0, keepdims=True))
        onehot = onehot + hit.astype(F32)
        cand = jnp.where(hit, neg, cand)
    wsum = wts[0]
    for k in range(1, TOP_K):
        wsum = wsum + wts[k]
    oh_b = onehot.astype(BF16)
    pre = jnp.dot(oh_b, tri_ref[...], preferred_element_type=F32) + carry_ref[:, 0:1]
    for k in range(TOP_K):
        e_ref[k:k + 1, :] = idxs[k]
        w_ref[k:k + 1, :] = wts[k] / wsum * ROUTE_SCALE
        r_ref[k:k + 1, :] = jnp.sum(jnp.where(eid == idxs[k], pre, 0.0), axis=0, keepdims=True).astype(I32)
    carry_ref[...] = carry_ref[...] + jnp.sum(onehot, axis=1, keepdims=True)
    rowc_ref[...] = rowc_ref[...] + dg(jnp.ones((8, tm), BF16), oh_b)
    cnt_ref[...] = carry_ref[...]
    crow_ref[...] = rowc_ref[...]


def _route(x, wt, bias, tm):
    t, d = x.shape
    ne = wt.shape[0]
    tri = jnp.asarray(np.triu(np.ones((tm, tm), np.float32), k=1), BF16)
    kt = pl.BlockSpec((TOP_K, tm), lambda i: (0, i))
    return pl.pallas_call(
        _route_body, grid=(t // tm,),
        in_specs=[pl.BlockSpec((tm, d), lambda i: (i, 0)), pl.BlockSpec((ne, d), lambda i: (0, 0)),
                  pl.BlockSpec((ne, 1), lambda i: (0, 0)), pl.BlockSpec((tm, tm), lambda i: (0, 0))],
        out_specs=[kt, kt, kt, pl.BlockSpec((ne, LANES), lambda i: (0, 0)), pl.BlockSpec((8, ne), lambda i: (0, 0))],
        out_shape=[jax.ShapeDtypeStruct((TOP_K, t), I32), jax.ShapeDtypeStruct((TOP_K, t), F32),
                   jax.ShapeDtypeStruct((TOP_K, t), I32), jax.ShapeDtypeStruct((ne, LANES), F32),
                   jax.ShapeDtypeStruct((8, ne), F32)],
        scratch_shapes=[pltpu.VMEM((ne, LANES), F32), pltpu.VMEM((8, ne), F32)],
        compiler_params=_cparams(("arbitrary",), 40), name="moe_route")(x, wt, bias, tri)


def _meta_body(cnt_ref, crow_ref, ps_ref, pe_ref, be_ref, nu_ref, *, rows, nbp):
    ne = cnt_ref.shape[0]
    pad = lambda c: jnp.floor((c + (rows - 1)) / rows) * rows
    prow = pad(crow_ref[0:1, :])
    pcol = pad(cnt_ref[:, 0:1])
    er = lax.broadcasted_iota(I32, (ne, ne), 0)
    ec = lax.broadcasted_iota(I32, (ne, ne), 1)
    pstart = jnp.sum(jnp.where(ec < er, prow, 0.0), axis=1, keepdims=True)
    pend = pstart + pcol
    ps_ref[...] = jnp.broadcast_to(pstart, ps_ref.shape).astype(I32)
    pe_ref[...] = jnp.broadcast_to(pend, pe_ref.shape).astype(I32)
    bstart = lax.broadcasted_iota(I32, (ne, nbp), 1).astype(F32) * rows
    be = jnp.sum((pend <= bstart).astype(F32), axis=0, keepdims=True)
    be_ref[...] = jnp.broadcast_to(jnp.minimum(be, ne - 1.0), be_ref.shape).astype(I32)
    total = jnp.sum(prow, axis=1, keepdims=True) / rows
    nu_ref[...] = jnp.broadcast_to(total, nu_ref.shape).astype(I32)


def _meta(cnt, crow, rows, nbp):
    ne = cnt.shape[0]
    return pl.pallas_call(
        functools.partial(_meta_body, rows=rows, nbp=nbp),
        out_shape=[jax.ShapeDtypeStruct((ne, LANES), I32), jax.ShapeDtypeStruct((ne, LANES), I32),
                   jax.ShapeDtypeStruct((8, nbp), I32), jax.ShapeDtypeStruct((8, LANES), I32)],
        name="moe_meta")(cnt, crow)


def _dest_body(e_ref, r_ref, ps_ref, d_ref):
    ne = ps_ref.shape[0]
    tt = e_ref.shape[1]
    eid = lax.broadcasted_iota(I32, (ne, tt), 0)
    ps = ps_ref[:, 0:1].astype(F32)
    for k in range(TOP_K):
        start = jnp.sum(jnp.where(eid == e_ref[k:k + 1, :], ps, 0.0), axis=0, keepdims=True)
        d_ref[k:k + 1, :] = start.astype(I32) + r_ref[k:k + 1, :]


def _dest(e, r, ps, tt):
    t = e.shape[1]
    kt = pl.BlockSpec((TOP_K, tt), lambda i: (0, i))
    return pl.pallas_call(
        _dest_body, grid=(t // tt,),
        in_specs=[kt, kt, pl.BlockSpec(ps.shape, lambda i: (0, 0))], out_specs=kt,
        out_shape=jax.ShapeDtypeStruct((TOP_K, t), I32),
        compiler_params=_cparams(("parallel",), 16), name="moe_dest")(e, r, ps)


def _dispatch_body(dest_ref, cnt_ref, pe_ref, xw_ref, xb_ref, zero_ref, sem, zsem, *, rows):
    i = pl.program_id(0)
    tt = xw_ref.shape[0]
    ne = pe_ref.shape[0]

    def zcopy(e):
        start = pl.multiple_of(pe_ref[e, 0] - rows, rows)
        return pltpu.make_async_copy(zero_ref, xb_ref.at[pl.ds(start, rows), :], zsem)

    @pl.when(i == 0)
    def _():
        zero_ref[...] = jnp.zeros(zero_ref.shape, zero_ref.dtype)

        def start(e, c):
            @pl.when(cnt_ref[e, 0] > 0)
            def _():
                zcopy(e).start()
            return c

        def wait(e, c):
            @pl.when(cnt_ref[e, 0] > 0)
            def _():
                zcopy(e).wait()
            return c

        lax.fori_loop(0, ne, start, 0)
        lax.fori_loop(0, ne, wait, 0)

    def row_copy(t, k):
        return pltpu.make_async_copy(xw_ref.at[pl.ds(t, 1), :], xb_ref.at[pl.ds(dest_ref[k, t], 1), :], sem)

    def issue(t, c):
        for k in range(TOP_K):
            row_copy(t, k).start()
        return c

    def drain(t, c):
        for k in range(TOP_K):
            row_copy(t, k).wait()
        return c

    lax.fori_loop(0, tt, issue, 0)
    lax.fori_loop(0, tt, drain, 0)


def _dispatch(dest, cnt_i, pend, xw, n_slots, rows, tt):
    t, hw = xw.shape
    smem = functools.partial(pl.BlockSpec, memory_space=pltpu.SMEM)
    return pl.pallas_call(
        functools.partial(_dispatch_body, rows=rows), grid=(t // tt,),
        in_specs=[smem((TOP_K, tt), lambda i: (0, i)), smem(cnt_i.shape, lambda i: (0, 0)),
                  smem(pend.shape, lambda i: (0, 0)), pl.BlockSpec((tt, hw), lambda i: (i, 0))],
        out_specs=pl.BlockSpec(memory_space=pl.ANY),
        out_shape=jax.ShapeDtypeStruct((n_slots, hw), U32),
        scratch_shapes=[pltpu.VMEM((rows, hw), U32), pltpu.SemaphoreType.DMA(()), pltpu.SemaphoreType.DMA(())],
        compiler_params=_cparams(("arbitrary",), 24), name="moe_dispatch")(dest, cnt_i, pend, xw)


def _silu_mul(g, u):
    return jax.nn.silu(g) * u


def _exp_up_body(be_ref, nu_ref, xb_ref, wg_ref, wu_ref, h_ref, wgb_ref, wub_ref):
    i = pl.program_id(0)
    valid = i < nu_ref[0]
    prev = be_ref[jnp.maximum(i - 1, 0)]
    fresh = (i == 0) | (be_ref[i] != prev)

    @pl.when(valid & fresh)
    def _():
        wgb_ref[...] = wg_ref[0].astype(BF16)
        wub_ref[...] = wu_ref[0].astype(BF16)

    @pl.when(valid)
    def _():
        hcols = xb_ref.shape[1]
        hi, lo = _unpack_halves(xb_ref[...])
        xa, xc = hi.astype(BF16), lo.astype(BF16)
        dot = lambda a, w: jnp.dot(a, w, preferred_element_type=F32)
        g = dot(xa, wgb_ref[:hcols]) + dot(xc, wgb_ref[hcols:])
        u = dot(xa, wub_ref[:hcols]) + dot(xc, wub_ref[hcols:])
        h_ref[...] = _silu_mul(g, u).astype(h_ref.dtype)

    @pl.when(jnp.logical_not(valid))
    def _():
        h_ref[...] = jnp.zeros(h_ref.shape, h_ref.dtype)


def _exp_up(bexp, nused, xb, wg, wu, rows):
    n_slots, hw = xb.shape
    ne, d, de = wg.shape
    nb = n_slots // rows
    last = lambda i, be, nu: jnp.minimum(i, nu[0] - 1)
    wspec = pl.BlockSpec((1, d, de), lambda i, be, nu: (be[last(i, be, nu)], 0, 0))
    return pl.pallas_call(
        _exp_up_body,
        grid_spec=pltpu.PrefetchScalarGridSpec(
            num_scalar_prefetch=2, grid=(nb,),
            in_specs=[pl.BlockSpec((rows, hw), lambda i, be, nu: (last(i, be, nu), 0)), wspec, wspec],
            out_specs=pl.BlockSpec((rows, de), lambda i, be, nu: (i, 0)),
            scratch_shapes=[pltpu.VMEM((d, de), BF16), pltpu.VMEM((d, de), BF16)]),
        out_shape=jax.ShapeDtypeStruct((n_slots, de), BF16),
        compiler_params=_cparams(("arbitrary",), 56), name="moe_expert_up")(bexp, nused, xb, wg, wu)


def _exp_down_body(be_ref, nu_ref, h_ref, wd_ref, y_ref, wdb_ref):
    i = pl.program_id(0)
    valid = i < nu_ref[0]
    prev = be_ref[jnp.maximum(i - 1, 0)]
    fresh = (i == 0) | (be_ref[i] != prev)

    @pl.when(valid & fresh)
    def _():
        wdb_ref[...] = wd_ref[0].astype(BF16)

    @pl.when(valid)
    def _():
        y = jnp.dot(h_ref[...], wdb_ref[...], preferred_element_type=F32)
        y_ref[...] = _pack_halves(y.astype(BF16))

    @pl.when(jnp.logical_not(valid))
    def _():
        y_ref[...] = jnp.zeros(y_ref.shape, y_ref.dtype)


def _exp_down(bexp, nused, h, wd, rows):
    n_slots, de = h.shape
    d = wd.shape[2]
    nb = n_slots // rows
    last = lambda i, be, nu: jnp.minimum(i, nu[0] - 1)
    return pl.pallas_call(
        _exp_down_body,
        grid_spec=pltpu.PrefetchScalarGridSpec(
            num_scalar_prefetch=2, grid=(nb,),
            in_specs=[pl.BlockSpec((rows, de), lambda i, be, nu: (last(i, be, nu), 0)),
                      pl.BlockSpec((1, de, d), lambda i, be, nu: (be[last(i, be, nu)], 0, 0))],
            out_specs=pl.BlockSpec((rows, d // 2), lambda i, be, nu: (i, 0)),
            scratch_shapes=[pltpu.VMEM((de, d), BF16)]),
        out_shape=jax.ShapeDtypeStruct((n_slots, d // 2), U32),
        compiler_params=_cparams(("arbitrary",), 40), name="moe_expert_down")(bexp, nused, h, wd)


def _shared_up_body(x_ref, wg_ref, wu_ref, o_ref):
    x = x_ref[...]
    g = jnp.dot(x, wg_ref[...], preferred_element_type=F32)
    u = jnp.dot(x, wu_ref[...], preferred_element_type=F32)
    o_ref[...] = _silu_mul(g, u).astype(o_ref.dtype)


def _shared_up(x, wg, wu, tm):
    t, d = x.shape
    de = wg.shape[1]
    wspec = pl.BlockSpec((d, de), lambda i: (0, 0))
    return pl.pallas_call(
        _shared_up_body, grid=(t // tm,),
        in_specs=[pl.BlockSpec((tm, d), lambda i: (i, 0)), wspec, wspec],
        out_specs=pl.BlockSpec((tm, de), lambda i: (i, 0)),
        out_shape=jax.ShapeDtypeStruct((t, de), BF16),
        compiler_params=_cparams(("parallel",), 48), name="moe_shared_up")(x, wg, wu)


def _combine_body(dcur_ref, dnext_ref, w_ref, x_ref, hs_ref, wd_ref, g_ref, b_ref, yb_ref,
                  o_ref, ob_ref, buf_ref, sem, *, nsteps):
    i = pl.program_id(0)
    tt = x_ref.shape[0]
    hcols = x_ref.shape[1] // 2
    slot = i % 2

    def row_copy(d_ref, s, t, k):
        return pltpu.make_async_copy(yb_ref.at[pl.ds(d_ref[k, t], 1), :], buf_ref.at[s, k, pl.ds(t, 1), :], sem.at[s])

    def issue(d_ref, s):
        def body(t, c):
            for k in range(TOP_K):
                row_copy(d_ref, s, t, k).start()
            return c
        lax.fori_loop(0, tt, body, 0)

    @pl.when(i == 0)
    def _():
        issue(dcur_ref, 0)

    @pl.when(i + 1 < nsteps)
    def _():
        issue(dnext_ref, 1 - slot)

    def drain(t, c):
        for k in range(TOP_K):
            row_copy(dcur_ref, slot, t, k).wait()
        return c
    lax.fori_loop(0, tt, drain, 0)

    acc = ALPHA * x_ref[...] + jnp.dot(hs_ref[...], wd_ref[...], preferred_element_type=F32)
    lo_acc = acc[:, hcols:]
    hi_acc = acc[:, :hcols]
    for k in range(TOP_K):
        gate = w_ref[:, k:k + 1]
        hi, lo = _unpack_halves(buf_ref[slot, k])
        hi_acc = hi_acc + gate * hi
        lo_acc = lo_acc + gate * lo
    o = _layernorm(jnp.concatenate([hi_acc, lo_acc], axis=1), g_ref[...], b_ref[...])
    o_ref[...] = o
    ob_ref[...] = o.astype(BF16)


def _combine(dest, gates, x1, hs, wd, g, b, yb, tt):
    t, d = x1.shape
    de = hs.shape[1]
    nsteps = t // tt
    smem = functools.partial(pl.BlockSpec, memory_space=pltpu.SMEM)
    row = pl.BlockSpec((tt, d), lambda i: (i, 0))
    vec = pl.BlockSpec((1, d), lambda i: (0, 0))
    return pl.pallas_call(
        functools.partial(_combine_body, nsteps=nsteps), grid=(nsteps,),
        in_specs=[smem((TOP_K, tt), lambda i: (0, i)),
                  smem((TOP_K, tt), lambda i: (0, jnp.minimum(i + 1, nsteps - 1))),
                  pl.BlockSpec((tt, TOP_K), lambda i: (i, 0)), row,
                  pl.BlockSpec((tt, de), lambda i: (i, 0)), pl.BlockSpec((de, d), lambda i: (0, 0)),
                  vec, vec, pl.BlockSpec(memory_space=pl.ANY)],
        out_specs=[row, row],
        out_shape=[jax.ShapeDtypeStruct((t, d), F32), jax.ShapeDtypeStruct((t, d), BF16)],
        scratch_shapes=[pltpu.VMEM((2, TOP_K, tt, d // 2), U32), pltpu.SemaphoreType.DMA((2,))],
        compiler_params=_cparams(("arbitrary",), 48), name="moe_combine_ln2")(
            dest, dest, gates, x1, hs, wd, g, b, yb)


def _moe_ln2(x1, x1b, xw, router_wt, router_bias, wg, wu, wd, shg, shu, shd, g2, b2):
    t, d = x1.shape
    ne = router_wt.shape[0]
    rows = MOE_ROWS
    nb = -(-(t * TOP_K + ne * (rows - 1)) // rows)
    nbp = -(-nb // LANES) * LANES
    eidx, gates, rank, cnt, crow = _route(x1, router_wt, router_bias, tm=512)
    pstart, pend, bexp, nused = _meta(cnt, crow, rows, nbp)
    dest = _dest(eidx, rank, pstart, tt=2048)
    xb = _dispatch(dest, cnt.astype(I32), pend, xw, nb * rows, rows, tt=256)
    hid = _exp_up(bexp[0], nused[0, :1], xb, wg, wu, rows)
    yb = _exp_down(bexp[0], nused[0, :1], hid, wd, rows)
    hs = _shared_up(x1b, shg, shu, tm=1024)
    return _combine(dest, gates.T, x1, hs, shd, g2, b2, yb, tt=128)


def kernel(x_prompt, x_sample, w_in, fnet_w, s5_a_re, s5_a_im, s5_log_dt, s5_b_re, s5_b_im, s5_c_re, s5_c_im,
           s5_d, glu_w, glu_b, gn_fnet, gn_s5, w_out, ln1_g, ln1_b, router_w, router_bias, exp_w_gate, exp_w_up,
           exp_w_down, sh_w_gate, sh_w_up, sh_w_down, ln2_g, ln2_b):
    bp, sp, d = x_prompt.shape
    bs, ss, _ = x_sample.shape
    depth = w_in.shape[0]
    heads, hd = fnet_w.shape[1], fnet_w.shape[2]
    fw = heads * hd
    n2 = 128
    seqs = [(0, bp, sp // n2, n2), (bp * sp, bs, ss // n2, n2)]
    seq_lens = [sp] * bp + [ss] * bs
    consts = {"chan": _chan_const(hd)}
    for (_, _, n1, n2_) in seqs:
        consts[(n1, n2_)] = _dft_consts(n1, n2_, hd)

    x = jnp.concatenate([x_prompt.reshape(bp * sp, d), x_sample.reshape(bs * ss, d)], axis=0)
    xb = x.astype(BF16)
    for l in range(depth):
        ops = _s5_prep(s5_a_re[l], s5_a_im[l], s5_log_dt[l], s5_b_re[l], s5_b_im[l], s5_c_re[l], s5_c_im[l], s5_d[l])
        h = _mm(xb, w_in[l].astype(BF16), tm=1024, tn=512, out_dtype=BF16, name="w_in")
        f = _fnet_branch(h, fnet_w[l].astype(BF16), seqs, consts, heads, hd)
        ys = _s5_branch(h, fw, ops, seq_lens)
        sm = _glu(ys, glu_w[l].astype(BF16), glu_b[l].reshape(1, -1), tm=1024, tn=1024)
        mix = _mixnorm(f, sm, gn_fnet[l].reshape(1, -1), gn_s5[l].reshape(1, -1), tm=512)
        y1 = _mm_res(mix, w_out[l].astype(BF16), x, tm=1024, tn=512, name="w_out")
        x1, x1b, xw = _ln1(y1, ln1_g[l].reshape(1, -1), ln1_b[l].reshape(1, -1), tm=256)
        x, xb = _moe_ln2(x1, x1b, xw, router_w[l].T, router_bias[l].reshape(-1, 1), exp_w_gate[l], exp_w_up[l],
                         exp_w_down[l], sh_w_gate[l].astype(BF16), sh_w_up[l].astype(BF16),
                         sh_w_down[l].astype(BF16), ln2_g[l].reshape(1, -1), ln2_b[l].reshape(1, -1))
    return (x[:bp * sp].reshape(bp, sp, d), x[bp * sp:].reshape(bs, ss, d))
```

```python
import functools
import math

import numpy as np
import jax
import jax.numpy as jnp
from jax import lax
from jax.experimental import pallas as pl
from jax.experimental.pallas import tpu as pltpu

F32 = jnp.float32
BF16 = jnp.bfloat16
I32 = jnp.int32
U32 = jnp.uint32

DEPTH = 4
FNET_HEADS = 8
S5_P = 16
S5_N = 64
N_EXPERTS = 64
N_EXP_GROUPS = 8
EXP_PER_GROUP = N_EXPERTS // N_EXP_GROUPS
TOPK_GROUPS = 4
TOP_K = 8
ROUTE_SCALE = 2.5
ALPHA = (2 * DEPTH) ** 0.25
EPS = 1e-5

S5_L = 16
S5_PAIR = 2
MOE_ROWS = 256
V7X_VMEM_BYTES = 64 * 1024 * 1024
VMEM_CAP = V7X_VMEM_BYTES - 8 * 1024 * 1024
LANES = 128


def _cparams(sem, vmem_mb):
    return pltpu.CompilerParams(dimension_semantics=sem,
                                vmem_limit_bytes=min(int(vmem_mb * 1024 * 1024), VMEM_CAP))


def _mm_body(x_ref, w_ref, o_ref):
    o_ref[...] = jnp.dot(x_ref[...], w_ref[...], preferred_element_type=F32).astype(o_ref.dtype)


def _mm(x, w, tm, tn, out_dtype, name):
    m, k = x.shape
    n = w.shape[1]
    return pl.pallas_call(
        _mm_body, grid=(m // tm, n // tn),
        in_specs=[pl.BlockSpec((tm, k), lambda i, j: (i, 0)), pl.BlockSpec((k, tn), lambda i, j: (0, j))],
        out_specs=pl.BlockSpec((tm, tn), lambda i, j: (i, j)),
        out_shape=jax.ShapeDtypeStruct((m, n), out_dtype),
        compiler_params=_cparams(("parallel", "parallel"), 48), name=name)(x, w)


def _mm_res_body(x_ref, w_ref, r_ref, o_ref):
    acc = jnp.dot(x_ref[...], w_ref[...], preferred_element_type=F32)
    o_ref[...] = ALPHA * r_ref[...] + acc


def _mm_res(x, w, res, tm, tn, name):
    m, k = x.shape
    n = w.shape[1]
    return pl.pallas_call(
        _mm_res_body, grid=(m // tm, n // tn),
        in_specs=[pl.BlockSpec((tm, k), lambda i, j: (i, 0)), pl.BlockSpec((k, tn), lambda i, j: (0, j)),
                  pl.BlockSpec((tm, tn), lambda i, j: (i, j))],
        out_specs=pl.BlockSpec((tm, tn), lambda i, j: (i, j)),
        out_shape=jax.ShapeDtypeStruct((m, n), F32),
        compiler_params=_cparams(("parallel", "parallel"), 48), name=name)(x, w, res)


def _glu_body(y_ref, w_ref, b_ref, yt_ref, o_ref):
    g = jnp.dot(y_ref[...], w_ref[...], preferred_element_type=F32) + b_ref[...]
    o_ref[...] = (yt_ref[...].astype(F32) * jax.nn.sigmoid(g)).astype(o_ref.dtype)


def _glu(y, w, b, tm, tn):
    m, k = y.shape
    n = w.shape[1]
    return pl.pallas_call(
        _glu_body, grid=(m // tm, n // tn),
        in_specs=[pl.BlockSpec((tm, k), lambda i, j: (i, 0)), pl.BlockSpec((k, tn), lambda i, j: (0, j)),
                  pl.BlockSpec((1, tn), lambda i, j: (0, j)), pl.BlockSpec((tm, tn), lambda i, j: (i, j))],
        out_specs=pl.BlockSpec((tm, tn), lambda i, j: (i, j)),
        out_shape=jax.ShapeDtypeStruct((m, n), BF16),
        compiler_params=_cparams(("parallel", "parallel"), 32), name="s5_glu")(y, w, b, y)


def _dft_consts(n1, n2, hd):
    n = n1 * n2
    k = np.arange(n1)
    ang1 = 2 * np.pi * np.outer(k, k) / n1
    c1, s1 = np.cos(ang1), np.sin(ang1)
    w1 = np.block([[c1, s1], [-s1, c1]])
    ang_t = 2 * np.pi * np.outer(np.arange(n2), np.arange(n1)) / n
    twr = np.broadcast_to(np.cos(ang_t)[:, :, None], (n2, n1, LANES))
    twi = np.broadcast_to(-np.sin(ang_t)[:, :, None], (n2, n1, LANES))
    k2 = np.arange(n2)
    ang2 = 2 * np.pi * np.outer(k2, k2) / n2
    w2 = np.concatenate([np.cos(ang2), np.sin(ang2)], axis=1) / math.sqrt(n)
    return (jnp.asarray(w1, BF16), jnp.asarray(np.ascontiguousarray(twr), F32),
            jnp.asarray(np.ascontiguousarray(twi), F32), jnp.asarray(w2, BF16))


def _chan_const(hd):
    c = np.arange(hd)
    ang = 2 * np.pi * np.outer(c, c) / hd
    return jnp.asarray(np.concatenate([np.cos(ang), -np.sin(ang)], axis=1) / math.sqrt(hd), BF16)


def _chan_body(x_ref, w_ref, zr_ref, zi_ref, *, hd):
    r = jnp.dot(x_ref[...], w_ref[...], preferred_element_type=F32)
    zr_ref[...] = r[:, :hd].astype(BF16)
    zi_ref[...] = r[:, hd:].astype(BF16)


def _fnet_chan(h, wc, heads, hd, tm):
    t = h.shape[0]
    shp = jax.ShapeDtypeStruct((t, heads * hd), BF16)
    return pl.pallas_call(
        functools.partial(_chan_body, hd=hd), grid=(t // tm, heads),
        in_specs=[pl.BlockSpec((tm, hd), lambda i, j: (i, j)), pl.BlockSpec((hd, 2 * hd), lambda i, j: (0, 0))],
        out_specs=[pl.BlockSpec((tm, hd), lambda i, j: (i, j))] * 2,
        out_shape=[shp, shp],
        compiler_params=_cparams(("parallel", "parallel"), 32), name="fnet_chan")(h, wc)


def _stage1_body(zr_ref, zi_ref, twr_ref, twi_ref, w_ref, yr_ref, yi_ref, *, n2t, n1, reps):
    w = w_ref[...]
    for j in range(n2t):
        z = jnp.concatenate([zr_ref[j], zi_ref[j]], axis=0)
        y = jnp.dot(w, z, preferred_element_type=F32)
        yr, yi = y[:n1], y[n1:]
        twr = jnp.concatenate([twr_ref[j]] * reps, axis=1)
        twi = jnp.concatenate([twi_ref[j]] * reps, axis=1)
        yr_ref[j] = (yr * twr - yi * twi).astype(BF16)
        yi_ref[j] = (yr * twi + yi * twr).astype(BF16)


def _fnet_stage1(zr, zi, w1, twr, twi, n2t, ct):
    nseq, n2, n1, c = zr.shape
    blk = pl.BlockSpec((None, n2t, n1, ct), lambda s, a, b: (s, a, 0, b))
    twb = pl.BlockSpec((n2t, n1, LANES), lambda s, a, b: (a, 0, 0))
    shp = jax.ShapeDtypeStruct(zr.shape, BF16)
    return pl.pallas_call(
        functools.partial(_stage1_body, n2t=n2t, n1=n1, reps=ct // LANES),
        grid=(nseq, n2 // n2t, c // ct),
        in_specs=[blk, blk, twb, twb, pl.BlockSpec((2 * n1, 2 * n1), lambda s, a, b: (0, 0))],
        out_specs=[blk, blk], out_shape=[shp, shp],
        compiler_params=_cparams(("parallel", "parallel", "parallel"), 32),
        name=f"fnet_stage1_n{n1}")(zr, zi, twr, twi, w1)


def _stage2_body(yr_ref, yi_ref, w_ref, wm_ref, f_ref, *, k1t):
    w = w_ref[...]
    wm = wm_ref[0]
    for j in range(k1t):
        yy = jnp.concatenate([yr_ref[j], yi_ref[j]], axis=0)
        r = jnp.dot(w, yy, preferred_element_type=F32)
        f_ref[j] = jnp.dot(r.astype(BF16), wm, preferred_element_type=F32).astype(BF16)


def _fnet_stage2(yr, yi, w2, wmap, k1t, hd):
    nseq, n1, n2, c = yr.shape
    blk = pl.BlockSpec((None, k1t, n2, hd), lambda s, a, h: (s, a, 0, h))
    return pl.pallas_call(
        functools.partial(_stage2_body, k1t=k1t), grid=(nseq, n1 // k1t, c // hd),
        in_specs=[blk, blk, pl.BlockSpec((n2, 2 * n2), lambda s, a, h: (0, 0)),
                  pl.BlockSpec((1, hd, hd), lambda s, a, h: (h, 0, 0))],
        out_specs=blk, out_shape=jax.ShapeDtypeStruct(yr.shape, BF16),
        compiler_params=_cparams(("parallel", "parallel", "parallel"), 32),
        name=f"fnet_stage2_n{n1}")(yr, yi, w2, wmap)


def _fnet_branch(h, wmap_b, seqs, consts, heads, hd):
    c = heads * hd
    zr, zi = _fnet_chan(h, consts["chan"], heads, hd, tm=2048)
    outs = []
    for (row0, nseq, n1, n2) in seqs:
        rows = nseq * n1 * n2
        w1, twr, twi, w2 = consts[(n1, n2)]

        def to_stage1(z):
            return z[row0:row0 + rows].reshape(nseq, n1, n2, c).transpose(0, 2, 1, 3)

        yr, yi = _fnet_stage1(to_stage1(zr), to_stage1(zi), w1, twr, twi, n2t=8, ct=min(c, 1024))
        yr = yr.transpose(0, 2, 1, 3)
        yi = yi.transpose(0, 2, 1, 3)
        f = _fnet_stage2(yr, yi, w2, wmap_b, k1t=8, hd=hd)
        outs.append(f.transpose(0, 2, 1, 3).reshape(rows, c))
    return jnp.concatenate(outs, axis=0)


def _s5_prep_body(are_ref, aim_ref, ldt_ref, brt_ref, bit_ref, cr_ref, ci_ref, crt_ref, cit_ref, dd_ref,
                  m_ref, p_ref, q_ref, lam_ref):
    L, P = S5_L, S5_P
    R = L * P
    lane = lax.broadcasted_iota(I32, (R, LANES), 1)
    member = lane // S5_N
    slot = lane // P
    row = lax.broadcasted_iota(I32, (L, LANES), 0).astype(F32)

    def powers(ar, ai, dt, tau):
        e = jnp.exp(tau * (ar * dt))
        ang = tau * (ai * dt)
        return e * jnp.cos(ang), e * jnp.sin(ang)

    def expand(x):
        return jnp.concatenate([jnp.broadcast_to(x[t:t + 1], (P, LANES)) for t in range(L)], axis=0)

    def tile(x):
        return jnp.concatenate([x] * L, axis=0)

    def cmul(ar_, ai_, br_, bi_):
        return ar_ * br_ - ai_ * bi_, ar_ * bi_ + ai_ * br_

    gs, qs, kts, lams = [], [], [], []
    for d in (0, 1):
        ar, ai = are_ref[d], aim_ref[d]
        dt = jnp.exp(ldt_ref[d])
        one = jnp.ones((1, LANES), F32)
        lr, li = powers(ar, ai, dt, one)
        den = ar * ar + ai * ai
        xr, xi = lr - 1.0, li
        cfr = (xr * ar + xi * ai) / den
        cfi = (xi * ar - xr * ai) / den
        bbr, bbi = cmul(cfr, cfi, brt_ref[d], bit_ref[d])
        tau_g = (L - 1.0 - row) if d == 0 else row
        tau_q = (row + 1.0) if d == 0 else (L - row)
        pr, pi = powers(ar, ai, dt, tau_g)
        gr, gi = cmul(expand(pr), expand(pi), tile(bbr), tile(bbi))
        qr, qi = powers(ar, ai, dt, tau_q)
        er, ei = cmul(tile(cr_ref[d]), tile(ci_ref[d]), expand(qr), expand(qi))
        gs.append((gr, gi))
        qs.append((er, -ei))
        lams.extend(powers(ar, ai, dt, float(L) * one))
        rhs = jnp.concatenate([crt_ref[d], cit_ref[d]], axis=0)
        per_member = []
        for g in range(S5_PAIR):
            msk = member == g
            lhs = jnp.concatenate([jnp.where(msk, gr, 0.0), jnp.where(msk, -gi, 0.0)], axis=1)
            per_member.append(jnp.dot(lhs, rhs, preferred_element_type=F32, precision=lax.Precision.HIGHEST))
        kts.append(per_member)

    m_ref[...] = jnp.zeros(m_ref.shape, m_ref.dtype)
    for g in range(S5_PAIR):
        ktf, ktb = kts[0][g], kts[1][g]
        v = jnp.concatenate([ktf[:R - P], ktf[R - P:] + ktb[:P] + dd_ref[g], ktb[P:]], axis=0)
        for hh in range(R // LANES):
            acc = jnp.zeros((R, LANES), F32)
            for il in range(LANES // P):
                i = hh * (LANES // P) + il
                acc = jnp.where(slot == il, v[(L - 1 - i) * P:(L - 1 - i) * P + R], acc)
            m_ref[g * R:(g + 1) * R, g * R + hh * LANES:g * R + (hh + 1) * LANES] = acc.astype(m_ref.dtype)
        msk = member == g
        comps_p = [gs[0][0], gs[0][1], gs[1][0], gs[1][1]]
        comps_q = [qs[0][0], qs[0][1], qs[1][0], qs[1][1]]
        for c in range(4):
            p_ref[g * R:(g + 1) * R, c * LANES:(c + 1) * LANES] = jnp.where(msk, comps_p[c], 0.0).astype(p_ref.dtype)
            q_ref[g * R:(g + 1) * R, c * LANES:(c + 1) * LANES] = jnp.where(msk, comps_q[c], 0.0).astype(q_ref.dtype)
    lam_ref[...] = jnp.concatenate(lams, axis=1)


def _s5_prep(a_re, a_im, log_dt, b_re, b_im, c_re, c_im, d_skip):
    g, n, p = b_re.shape[1], b_re.shape[2], b_re.shape[3]
    sg = g // S5_PAIR
    assert n == S5_N and p == S5_P and S5_PAIR * n == LANES
    lanes4 = lambda x: x.reshape(2, sg, 1, LANES).astype(F32)
    are, aim = lanes4(a_re), lanes4(a_im)
    ldt = lanes4(jnp.repeat(log_dt, n, axis=1))
    bt = lambda x: x.reshape(2, sg, S5_PAIR, n, p).transpose(0, 1, 4, 2, 3).reshape(2, sg, p, LANES).astype(F32)
    cc = lambda x: x.reshape(2, sg, S5_PAIR, p, n).transpose(0, 1, 3, 2, 4).reshape(2, sg, p, LANES).astype(F32)
    ct = lambda x: jnp.tile(x.reshape(2, sg, S5_PAIR, p, n).transpose(0, 1, 2, 4, 3).reshape(2, sg, LANES, p),
                            (1, 1, 1, LANES // p)).astype(F32)
    eye = jnp.eye(p, dtype=F32)
    dd = d_skip.reshape(sg, S5_PAIR, 1, p).astype(F32) * eye[None, None]
    dd = jnp.tile(dd, (1, 1, 1, LANES // p))
    r2 = S5_PAIR * S5_L * p
    v1 = pl.BlockSpec((2, None, 1, LANES), lambda s: (0, s, 0, 0))
    vp = pl.BlockSpec((2, None, p, LANES), lambda s: (0, s, 0, 0))
    vt = pl.BlockSpec((2, None, LANES, LANES), lambda s: (0, s, 0, 0))
    mat = pl.BlockSpec((None, r2, r2), lambda s: (s, 0, 0))
    mshape = jax.ShapeDtypeStruct((sg, r2, r2), BF16)
    return pl.pallas_call(
        _s5_prep_body, grid=(sg,),
        in_specs=[v1, v1, v1, vp, vp, vp, vp, vt, vt, pl.BlockSpec((None, S5_PAIR, p, LANES), lambda s: (s, 0, 0, 0))],
        out_specs=[mat, mat, mat, pl.BlockSpec((None, 1, 4 * LANES), lambda s: (s, 0, 0))],
        out_shape=[mshape, mshape, mshape, jax.ShapeDtypeStruct((sg, 1, 4 * LANES), F32)],
        compiler_params=_cparams(("parallel",), 32), name="s5_prep")(
            are, aim, ldt, bt(b_re), bt(b_im), cc(c_re), cc(c_im), ct(c_re), ct(c_im), dd)


def _s5_statein(u, pm, nct):
    sg, nc, r2 = u.shape
    return pl.pallas_call(
        _mm_body, grid=(sg, nc // nct),
        in_specs=[pl.BlockSpec((None, nct, r2), lambda s, c: (s, c, 0)),
                  pl.BlockSpec((None, r2, pm.shape[2]), lambda s, c: (s, 0, 0))],
        out_specs=pl.BlockSpec((None, nct, pm.shape[2]), lambda s, c: (s, c, 0)),
        out_shape=jax.ShapeDtypeStruct((sg, nc, pm.shape[2]), F32),
        compiler_params=_cparams(("parallel", "parallel"), 32), name="s5_statein")(u, pm)


def _s5_scan_body(fk_ref, bk_ref, rs_ref, pf_ref, pb_ref, lam_ref, sf_ref, sb_ref, st_ref, *, nct):
    k = pl.program_id(0)

    @pl.when(rs_ref[k] == 1)
    def _():
        st_ref[...] = jnp.zeros(st_ref.shape, F32)

    lam = lam_ref[...]
    lfr, lfi = lam[:, 0:LANES], lam[:, LANES:2 * LANES]
    lbr, lbi = lam[:, 2 * LANES:3 * LANES], lam[:, 3 * LANES:]

    def step(i, carry):
        fr, fi, br, bi = carry
        cb = nct - 1 - i
        sf_ref[i] = jnp.concatenate([fr, fi], axis=1).astype(BF16)
        sb_ref[cb] = jnp.concatenate([br, bi], axis=1).astype(BF16)
        pf = pf_ref[i]
        pb = pb_ref[cb]
        fr2 = lfr * fr - lfi * fi + pf[:, :LANES]
        fi2 = lfr * fi + lfi * fr + pf[:, LANES:]
        br2 = lbr * br - lbi * bi + pb[:, :LANES]
        bi2 = lbr * bi + lbi * br + pb[:, LANES:]
        return fr2, fi2, br2, bi2

    out = lax.fori_loop(0, nct, step, (st_ref[0], st_ref[1], st_ref[2], st_ref[3]))
    for j in range(4):
        st_ref[j] = out[j]


def _s5_scan(pin, lam, seq_chunks, nct):
    nc, sg, _ = pin.shape
    fk, bk, rs = [], [], []
    base = 0
    for n_chunks in seq_chunks:
        nb = n_chunks // nct
        for j in range(nb):
            fk.append(base + j)
            bk.append(base + nb - 1 - j)
            rs.append(1 if j == 0 else 0)
        base += nb
    nsteps = len(fk)
    to = lambda v: jnp.asarray(np.asarray(v, np.int32))
    blk_f = pl.BlockSpec((nct, sg, 2 * LANES), lambda k, fk_, bk_, rs_: (fk_[k], 0, 0))
    blk_b = pl.BlockSpec((nct, sg, 2 * LANES), lambda k, fk_, bk_, rs_: (bk_[k], 0, 1))
    out_f = pl.BlockSpec((nct, sg, 2 * LANES), lambda k, fk_, bk_, rs_: (fk_[k], 0, 0))
    out_b = pl.BlockSpec((nct, sg, 2 * LANES), lambda k, fk_, bk_, rs_: (bk_[k], 0, 0))
    shp = jax.ShapeDtypeStruct((nc, sg, 2 * LANES), BF16)
    return pl.pallas_call(
        functools.partial(_s5_scan_body, nct=nct),
        grid_spec=pltpu.PrefetchScalarGridSpec(
            num_scalar_prefetch=3, grid=(nsteps,),
            in_specs=[blk_f, blk_b, pl.BlockSpec((sg, 4 * LANES), lambda k, *_: (0, 0))],
            out_specs=[out_f, out_b],
            scratch_shapes=[pltpu.VMEM((4, sg, LANES), F32)]),
        out_shape=[shp, shp],
        compiler_params=_cparams(("arbitrary",), 40), name="s5_scan")(to(fk), to(bk), to(rs), pin, pin, lam)


def _s5_out_body(u_ref, m_ref, q_ref, sf_ref, sb_ref, y_ref):
    y = jnp.dot(u_ref[...], m_ref[...], preferred_element_type=F32)
    s = jnp.concatenate([sf_ref[...], sb_ref[...]], axis=1)
    y = y + lax.dot_general(s, q_ref[...], (((1,), (1,)), ((), ())), preferred_element_type=F32)
    y_ref[...] = jax.nn.gelu(y).astype(y_ref.dtype)


def _s5_out(u, mm, qt, sf, sb, nct):
    sg, nc, r2 = u.shape
    ublk = pl.BlockSpec((None, nct, r2), lambda s, c: (s, c, 0))
    sblk = pl.BlockSpec((None, nct, 2 * LANES), lambda s, c: (s, c, 0))
    mblk = pl.BlockSpec((None, r2, r2), lambda s, c: (s, 0, 0))
    return pl.pallas_call(
        _s5_out_body, grid=(sg, nc // nct),
        in_specs=[ublk, mblk, mblk, sblk, sblk], out_specs=ublk,
        out_shape=jax.ShapeDtypeStruct(u.shape, BF16),
        compiler_params=_cparams(("parallel", "parallel"), 32), name="s5_out")(u, mm, qt, sf, sb)


def _s5_perm_const():
    half = S5_L // 2
    ppl = LANES // (S5_PAIR * S5_P)
    n = half * LANES
    e = np.zeros((n, n), np.float32)
    for sl in range(half):
        for pr in range(ppl):
            for g in range(S5_PAIR):
                for q in range(S5_P):
                    src = sl * LANES + pr * S5_PAIR * S5_P + g * S5_P + q
                    dst = (pr * S5_PAIR + g) * LANES + sl * S5_P + q
                    e[src, dst] = 1.0
    return e


def _s5_flatten_body(x_ref, e_ref, u_ref, scr_ref, *, nct):
    half = S5_L // 2
    r = S5_L * S5_P
    scr_ref[...] = x_ref[...].astype(F32).reshape(nct * S5_L, LANES)
    e = e_ref[...]
    for hh in range(2):
        pieces = [scr_ref[pl.ds(hh * half + sl, nct, stride=S5_L), :].astype(BF16) for sl in range(half)]
        res = jnp.dot(jnp.concatenate(pieces, axis=1), e, preferred_element_type=F32).astype(BF16)
        for j in range(LANES // S5_P):
            pr, g = divmod(j, S5_PAIR)
            u_ref[pr, :, g * r + hh * LANES:g * r + (hh + 1) * LANES] = res[:, j * LANES:(j + 1) * LANES]


def _s5_flatten(h, col0, w, nct):
    t = h.shape[0]
    nc = t // S5_L
    ppl = LANES // (S5_PAIR * S5_P)
    sg = w // (S5_PAIR * S5_P)
    r2 = S5_PAIR * S5_L * S5_P
    e = jnp.asarray(_s5_perm_const(), BF16)
    return pl.pallas_call(
        functools.partial(_s5_flatten_body, nct=nct), grid=(nc // nct, w // LANES),
        in_specs=[pl.BlockSpec((nct, S5_L, LANES), lambda c, l: (c, 0, col0 // LANES + l)),
                  pl.BlockSpec(e.shape, lambda c, l: (0, 0))],
        out_specs=pl.BlockSpec((ppl, nct, r2), lambda c, l: (l, c, 0)),
        out_shape=jax.ShapeDtypeStruct((sg, nc, r2), BF16),
        scratch_shapes=[pltpu.VMEM((nct * S5_L, LANES), F32)],
        compiler_params=_cparams(("parallel", "parallel"), 32), name="s5_flatten")(
            h.reshape(nc, S5_L, h.shape[1]), e)


def _s5_unflatten_body(y_ref, et_ref, o_ref, scr_ref, *, nct):
    half = S5_L // 2
    r = S5_L * S5_P
    et = et_ref[...]
    for hh in range(2):
        parts = []
        for j in range(LANES // S5_P):
            pr, g = divmod(j, S5_PAIR)
            parts.append(y_ref[pr, :, g * r + hh * LANES:g * r + (hh + 1) * LANES])
        res = jnp.dot(jnp.concatenate(parts, axis=1), et, preferred_element_type=F32)
        for sl in range(half):
            scr_ref[pl.ds(hh * half + sl, nct, stride=S5_L), :] = res[:, sl * LANES:(sl + 1) * LANES]
    o_ref[...] = scr_ref[...].reshape(nct, S5_L, LANES).astype(o_ref.dtype)


def _s5_unflatten(yg, nct):
    sg, nc, r2 = yg.shape
    ppl = LANES // (S5_PAIR * S5_P)
    w = sg * S5_PAIR * S5_P
    et = jnp.asarray(_s5_perm_const().T, BF16)
    out = pl.pallas_call(
        functools.partial(_s5_unflatten_body, nct=nct), grid=(nc // nct, w // LANES),
        in_specs=[pl.BlockSpec((ppl, nct, r2), lambda c, l: (l, c, 0)), pl.BlockSpec(et.shape, lambda c, l: (0, 0))],
        out_specs=pl.BlockSpec((nct, S5_L, LANES), lambda c, l: (c, 0, l)),
        out_shape=jax.ShapeDtypeStruct((nc, S5_L, w), BF16),
        scratch_shapes=[pltpu.VMEM((nct * S5_L, LANES), F32)],
        compiler_params=_cparams(("parallel", "parallel"), 32), name="s5_unflatten")(yg, et)
    return out.reshape(nc * S5_L, w)


def _s5_branch(h, col0, ops, seq_lens):
    mm, pm, qt, lam = ops
    t = h.shape[0]
    sg = mm.shape[0]
    w = sg * S5_PAIR * S5_P
    nc = t // S5_L
    u = _s5_flatten(h, col0, w, nct=256)
    pin = _s5_statein(u, pm, nct=nc // 2).transpose(1, 0, 2)
    sf, sb = _s5_scan(pin, lam.reshape(sg, 4 * LANES), [s // S5_L for s in seq_lens], nct=64)
    yg = _s5_out(u, mm, qt, sf.transpose(1, 0, 2), sb.transpose(1, 0, 2), nct=nc // 2)
    return _s5_unflatten(yg, nct=256)


def _mixnorm_body(f_ref, s_ref, gf_ref, gs_ref, o_ref, *, w):
    def rn(x, g):
        xf = x.astype(F32)
        return (xf * lax.rsqrt(jnp.mean(xf * xf, axis=-1, keepdims=True) + EPS) * g).astype(o_ref.dtype)
    o_ref[:, :w] = rn(f_ref[...], gf_ref[...])
    o_ref[:, w:] = rn(s_ref[...], gs_ref[...])


def _mixnorm(f, s, gf, gs, tm):
    t, w = f.shape
    return pl.pallas_call(
        functools.partial(_mixnorm_body, w=w), grid=(t // tm,),
        in_specs=[pl.BlockSpec((tm, w), lambda i: (i, 0)), pl.BlockSpec((tm, s.shape[1]), lambda i: (i, 0)),
                  pl.BlockSpec((1, w), lambda i: (0, 0)), pl.BlockSpec((1, s.shape[1]), lambda i: (0, 0))],
        out_specs=pl.BlockSpec((tm, w + s.shape[1]), lambda i: (i, 0)),
        out_shape=jax.ShapeDtypeStruct((t, w + s.shape[1]), BF16),
        compiler_params=_cparams(("parallel",), 32), name="mix_norm")(f, s, gf, gs)


def _layernorm(y, g, b):
    mu = jnp.mean(y, axis=-1, keepdims=True)
    yc = y - mu
    var = jnp.mean(yc * yc, axis=-1, keepdims=True)
    return yc * lax.rsqrt(var + EPS) * g + b


def _pack_halves(ob):
    hcols = ob.shape[1] // 2
    hi = lax.bitcast_convert_type(ob[:, :hcols].astype(F32), U32)
    lo = lax.bitcast_convert_type(ob[:, hcols:].astype(F32), U32)
    return hi | (lo >> 16)


def _unpack_halves(words):
    hi = lax.bitcast_convert_type(words & jnp.uint32(0xFFFF0000), F32)
    lo = lax.bitcast_convert_type(words << 16, F32)
    return hi, lo


def _ln1_body(y_ref, g_ref, b_ref, x_ref, xb_ref, xw_ref):
    o = _layernorm(y_ref[...], g_ref[...], b_ref[...])
    x_ref[...] = o
    ob = o.astype(BF16)
    xb_ref[...] = ob
    xw_ref[...] = _pack_halves(ob)


def _ln1(y, g, b, tm):
    t, d = y.shape
    row = pl.BlockSpec((tm, d), lambda i: (i, 0))
    vec = pl.BlockSpec((1, d), lambda i: (0, 0))
    return pl.pallas_call(
        _ln1_body, grid=(t // tm,), in_specs=[row, vec, vec],
        out_specs=[row, row, pl.BlockSpec((tm, d // 2), lambda i: (i, 0))],
        out_shape=[jax.ShapeDtypeStruct((t, d), F32), jax.ShapeDtypeStruct((t, d), BF16),
                   jax.ShapeDtypeStruct((t, d // 2), U32)],
        compiler_params=_cparams(("parallel",), 48), name="ln1")(y, g, b)


def _route_body(x_ref, wt_ref, bias_ref, tri_ref, e_ref, w_ref, r_ref, cnt_ref, crow_ref, carry_ref, rowc_ref):
    i = pl.program_id(0)

    @pl.when(i == 0)
    def _():
        carry_ref[...] = jnp.zeros(carry_ref.shape, F32)
        rowc_ref[...] = jnp.zeros(rowc_ref.shape, F32)

    ne = wt_ref.shape[0]
    tm = x_ref.shape[0]
    x = x_ref[...]
    xh = x.astype(BF16)
    xl = (x - xh.astype(F32)).astype(BF16)
    wt = wt_ref[...]
    wh = wt.astype(BF16)
    wl = (wt - wh.astype(F32)).astype(BF16)
    dn = (((1,), (1,)), ((), ()))
    dg = lambda a, b: lax.dot_general(a, b, dn, preferred_element_type=F32)
    logits = dg(wh, xh) + (dg(wh, xl) + dg(wl, xh))
    scores = jax.nn.sigmoid(logits)
    sel = scores + bias_ref[...]
    neg = jnp.float32(-jnp.inf)

    sub = lax.broadcasted_iota(I32, (EXP_PER_GROUP, tm), 0)
    gscore = []
    for g in range(N_EXP_GROUPS):
        blk = sel[g * EXP_PER_GROUP:(g + 1) * EXP_PER_GROUP]
        m1 = jnp.max(blk, axis=0, keepdims=True)
        i1 = jnp.min(jnp.where(blk == m1, sub, EXP_PER_GROUP), axis=0, keepdims=True)
        m2 = jnp.max(jnp.where(sub == i1, neg, blk), axis=0, keepdims=True)
        gscore.append(m1 + m2)
    keep = []
    for g in range(N_EXP_GROUPS):
        beat = jnp.zeros((1, tm), I32)
        for g2 in range(N_EXP_GROUPS):
            if g2 == g:
                continue
            wins = (gscore[g2] > gscore[g]) | ((gscore[g2] == gscore[g]) & (g2 < g))
            beat = beat + wins.astype(I32)
        keep.append(beat < TOPK_GROUPS)
    cand = jnp.concatenate(
        [jnp.where(keep[g], sel[g * EXP_PER_GROUP:(g + 1) * EXP_PER_GROUP], neg) for g in range(N_EXP_GROUPS)],
        axis=0)
    eid = lax.broadcasted_iota(I32, (ne, tm), 0)
    onehot = jnp.zeros((ne, tm), F32)
    idxs, wts = [], []
    for _ in range(TOP_K):
        m = jnp.max(cand, axis=0, keepdims=True)
        idx = jnp.min(jnp.where(cand == m, eid, ne), axis=0, keepdims=True)
        hit = eid == idx
        idxs.append(idx)
        wts.append(jnp.sum(jnp.where(hit, scores, 0.0), axis=0, keepdims=True))
        onehot = onehot + hit.astype(F32)
        cand = jnp.where(hit, neg, cand)
    wsum = wts[0]
    for k in range(1, TOP_K):
        wsum = wsum + wts[k]
    oh_b = onehot.astype(BF16)
    pre = jnp.dot(oh_b, tri_ref[...], preferred_element_type=F32) + carry_ref[:, 0:1]
    for k in range(TOP_K):
        e_ref[k:k + 1, :] = idxs[k]
        w_ref[k:k + 1, :] = wts[k] / wsum * ROUTE_SCALE
        r_ref[k:k + 1, :] = jnp.sum(jnp.where(eid == idxs[k], pre, 0.0), axis=0, keepdims=True).astype(I32)
    carry_ref[...] = carry_ref[...] + jnp.sum(onehot, axis=1, keepdims=True)
    rowc_ref[...] = rowc_ref[...] + dg(jnp.ones((8, tm), BF16), oh_b)
    cnt_ref[...] = carry_ref[...]
    crow_ref[...] = rowc_ref[...]


def _route(x, wt, bias, tm):
    t, d = x.shape
    ne = wt.shape[0]
    tri = jnp.asarray(np.triu(np.ones((tm, tm), np.float32), k=1), BF16)
    kt = pl.BlockSpec((TOP_K, tm), lambda i: (0, i))
    return pl.pallas_call(
        _route_body, grid=(t // tm,),
        in_specs=[pl.BlockSpec((tm, d), lambda i: (i, 0)), pl.BlockSpec((ne, d), lambda i: (0, 0)),
                  pl.BlockSpec((ne, 1), lambda i: (0, 0)), pl.BlockSpec((tm, tm), lambda i: (0, 0))],
        out_specs=[kt, kt, kt, pl.BlockSpec((ne, LANES), lambda i: (0, 0)), pl.BlockSpec((8, ne), lambda i: (0, 0))],
        out_shape=[jax.ShapeDtypeStruct((TOP_K, t), I32), jax.ShapeDtypeStruct((TOP_K, t), F32),
                   jax.ShapeDtypeStruct((TOP_K, t), I32), jax.ShapeDtypeStruct((ne, LANES), F32),
                   jax.ShapeDtypeStruct((8, ne), F32)],
        scratch_shapes=[pltpu.VMEM((ne, LANES), F32), pltpu.VMEM((8, ne), F32)],
        compiler_params=_cparams(("arbitrary",), 40), name="moe_route")(x, wt, bias, tri)


def _meta_body(cnt_ref, crow_ref, ps_ref, pe_ref, be_ref, nu_ref, *, rows, nbp):
    ne = cnt_ref.shape[0]
    pad = lambda c: jnp.floor((c + (rows - 1)) / rows) * rows
    prow = pad(crow_ref[0:1, :])
    pcol = pad(cnt_ref[:, 0:1])
    er = lax.broadcasted_iota(I32, (ne, ne), 0)
    ec = lax.broadcasted_iota(I32, (ne, ne), 1)
    pstart = jnp.sum(jnp.where(ec < er, prow, 0.0), axis=1, keepdims=True)
    pend = pstart + pcol
    ps_ref[...] = jnp.broadcast_to(pstart, ps_ref.shape).astype(I32)
    pe_ref[...] = jnp.broadcast_to(pend, pe_ref.shape).astype(I32)
    bstart = lax.broadcasted_iota(I32, (ne, nbp), 1).astype(F32) * rows
    be = jnp.sum((pend <= bstart).astype(F32), axis=0, keepdims=True)
    be_ref[...] = jnp.broadcast_to(jnp.minimum(be, ne - 1.0), be_ref.shape).astype(I32)
    total = jnp.sum(prow, axis=1, keepdims=True) / rows
    nu_ref[...] = jnp.broadcast_to(total, nu_ref.shape).astype(I32)


def _meta(cnt, crow, rows, nbp):
    ne = cnt.shape[0]
    return pl.pallas_call(
        functools.partial(_meta_body, rows=rows, nbp=nbp),
        out_shape=[jax.ShapeDtypeStruct((ne, LANES), I32), jax.ShapeDtypeStruct((ne, LANES), I32),
                   jax.ShapeDtypeStruct((8, nbp), I32), jax.ShapeDtypeStruct((8, LANES), I32)],
        name="moe_meta")(cnt, crow)


def _dest_body(e_ref, r_ref, ps_ref, d_ref):
    ne = ps_ref.shape[0]
    tt = e_ref.shape[1]
    eid = lax.broadcasted_iota(I32, (ne, tt), 0)
    ps = ps_ref[:, 0:1].astype(F32)
    for k in range(TOP_K):
        start = jnp.sum(jnp.where(eid == e_ref[k:k + 1, :], ps, 0.0), axis=0, keepdims=True)
        d_ref[k:k + 1, :] = start.astype(I32) + r_ref[k:k + 1, :]


def _dest(e, r, ps, tt):
    t = e.shape[1]
    kt = pl.BlockSpec((TOP_K, tt), lambda i: (0, i))
    return pl.pallas_call(
        _dest_body, grid=(t // tt,),
        in_specs=[kt, kt, pl.BlockSpec(ps.shape, lambda i: (0, 0))], out_specs=kt,
        out_shape=jax.ShapeDtypeStruct((TOP_K, t), I32),
        compiler_params=_cparams(("parallel",), 16), name="moe_dest")(e, r, ps)


def _dispatch_body(dest_ref, cnt_ref, pe_ref, xw_ref, xb_ref, zero_ref, sem, zsem, *, rows):
    i = pl.program_id(0)
    tt = xw_ref.shape[0]
    ne = pe_ref.shape[0]

    def zcopy(e):
        start = pl.multiple_of(pe_ref[e, 0] - rows, rows)
        return pltpu.make_async_copy(zero_ref, xb_ref.at[pl.ds(start, rows), :], zsem)

    @pl.when(i == 0)
    def _():
        zero_ref[...] = jnp.zeros(zero_ref.shape, zero_ref.dtype)

        def start(e, c):
            @pl.when(cnt_ref[e, 0] > 0)
            def _():
                zcopy(e).start()
            return c

        def wait(e, c):
            @pl.when(cnt_ref[e, 0] > 0)
            def _():
                zcopy(e).wait()
            return c

        lax.fori_loop(0, ne, start, 0)
        lax.fori_loop(0, ne, wait, 0)

    def row_copy(t, k):
        return pltpu.make_async_copy(xw_ref.at[pl.ds(t, 1), :], xb_ref.at[pl.ds(dest_ref[k, t], 1), :], sem)

    def issue(t, c):
        for k in range(TOP_K):
            row_copy(t, k).start()
        return c

    def drain(t, c):
        for k in range(TOP_K):
            row_copy(t, k).wait()
        return c

    lax.fori_loop(0, tt, issue, 0)
    lax.fori_loop(0, tt, drain, 0)


def _dispatch(dest, cnt_i, pend, xw, n_slots, rows, tt):
    t, hw = xw.shape
    smem = functools.partial(pl.BlockSpec, memory_space=pltpu.SMEM)
    return pl.pallas_call(
        functools.partial(_dispatch_body, rows=rows), grid=(t // tt,),
        in_specs=[smem((TOP_K, tt), lambda i: (0, i)), smem(cnt_i.shape, lambda i: (0, 0)),
                  smem(pend.shape, lambda i: (0, 0)), pl.BlockSpec((tt, hw), lambda i: (i, 0))],
        out_specs=pl.BlockSpec(memory_space=pl.ANY),
        out_shape=jax.ShapeDtypeStruct((n_slots, hw), U32),
        scratch_shapes=[pltpu.VMEM((rows, hw), U32), pltpu.SemaphoreType.DMA(()), pltpu.SemaphoreType.DMA(())],
        compiler_params=_cparams(("arbitrary",), 24), name="moe_dispatch")(dest, cnt_i, pend, xw)


def _silu_mul(g, u):
    return jax.nn.silu(g) * u


def _exp_up_body(be_ref, nu_ref, xb_ref, wg_ref, wu_ref, h_ref, wgb_ref, wub_ref):
    i = pl.program_id(0)
    valid = i < nu_ref[0]
    prev = be_ref[jnp.maximum(i - 1, 0)]
    fresh = (i == 0) | (be_ref[i] != prev)

    @pl.when(valid & fresh)
    def _():
        wgb_ref[...] = wg_ref[0].astype(BF16)
        wub_ref[...] = wu_ref[0].astype(BF16)

    @pl.when(valid)
    def _():
        hcols = xb_ref.shape[1]
        hi, lo = _unpack_halves(xb_ref[...])
        xa, xc = hi.astype(BF16), lo.astype(BF16)
        dot = lambda a, w: jnp.dot(a, w, preferred_element_type=F32)
        g = dot(xa, wgb_ref[:hcols]) + dot(xc, wgb_ref[hcols:])
        u = dot(xa, wub_ref[:hcols]) + dot(xc, wub_ref[hcols:])
        h_ref[...] = _silu_mul(g, u).astype(h_ref.dtype)

    @pl.when(jnp.logical_not(valid))
    def _():
        h_ref[...] = jnp.zeros(h_ref.shape, h_ref.dtype)


def _exp_up(bexp, nused, xb, wg, wu, layer, rows):
    n_slots, hw = xb.shape
    _, ne, d, de = wg.shape
    nb = n_slots // rows
    last = lambda i, be, nu: jnp.minimum(i, nu[0] - 1)
    wspec = pl.BlockSpec((None, 1, d, de), lambda i, be, nu: (layer, be[last(i, be, nu)], 0, 0))
    return pl.pallas_call(
        _exp_up_body,
        grid_spec=pltpu.PrefetchScalarGridSpec(
            num_scalar_prefetch=2, grid=(nb,),
            in_specs=[pl.BlockSpec((rows, hw), lambda i, be, nu: (last(i, be, nu), 0)), wspec, wspec],
            out_specs=pl.BlockSpec((rows, de), lambda i, be, nu: (i, 0)),
            scratch_shapes=[pltpu.VMEM((d, de), BF16), pltpu.VMEM((d, de), BF16)]),
        out_shape=jax.ShapeDtypeStruct((n_slots, de), BF16),
        compiler_params=_cparams(("arbitrary",), 56), name="moe_expert_up")(bexp, nused, xb, wg, wu)


def _exp_down_body(be_ref, nu_ref, h_ref, wd_ref, y_ref, wdb_ref):
    i = pl.program_id(0)
    valid = i < nu_ref[0]
    prev = be_ref[jnp.maximum(i - 1, 0)]
    fresh = (i == 0) | (be_ref[i] != prev)

    @pl.when(valid & fresh)
    def _():
        wdb_ref[...] = wd_ref[0].astype(BF16)

    @pl.when(valid)
    def _():
        y = jnp.dot(h_ref[...], wdb_ref[...], preferred_element_type=F32)
        y_ref[...] = _pack_halves(y.astype(BF16))

    @pl.when(jnp.logical_not(valid))
    def _():
        y_ref[...] = jnp.zeros(y_ref.shape, y_ref.dtype)


def _exp_down(bexp, nused, h, wd, layer, rows):
    n_slots, de = h.shape
    d = wd.shape[3]
    nb = n_slots // rows
    last = lambda i, be, nu: jnp.minimum(i, nu[0] - 1)
    return pl.pallas_call(
        _exp_down_body,
        grid_spec=pltpu.PrefetchScalarGridSpec(
            num_scalar_prefetch=2, grid=(nb,),
            in_specs=[pl.BlockSpec((rows, de), lambda i, be, nu: (last(i, be, nu), 0)),
                      pl.BlockSpec((None, 1, de, d), lambda i, be, nu: (layer, be[last(i, be, nu)], 0, 0))],
            out_specs=pl.BlockSpec((rows, d // 2), lambda i, be, nu: (i, 0)),
            scratch_shapes=[pltpu.VMEM((de, d), BF16)]),
        out_shape=jax.ShapeDtypeStruct((n_slots, d // 2), U32),
        compiler_params=_cparams(("arbitrary",), 40), name="moe_expert_down")(bexp, nused, h, wd)


def _shared_up_body(x_ref, wg_ref, wu_ref, o_ref):
    x = x_ref[...]
    g = jnp.dot(x, wg_ref[...], preferred_element_type=F32)
    u = jnp.dot(x, wu_ref[...], preferred_element_type=F32)
    o_ref[...] = _silu_mul(g, u).astype(o_ref.dtype)


def _shared_up(x, wg, wu, tm):
    t, d = x.shape
    de = wg.shape[1]
    wspec = pl.BlockSpec((d, de), lambda i: (0, 0))
    return pl.pallas_call(
        _shared_up_body, grid=(t // tm,),
        in_specs=[pl.BlockSpec((tm, d), lambda i: (i, 0)), wspec, wspec],
        out_specs=pl.BlockSpec((tm, de), lambda i: (i, 0)),
        out_shape=jax.ShapeDtypeStruct((t, de), BF16),
        compiler_params=_cparams(("parallel",), 48), name="moe_shared_up")(x, wg, wu)


def _combine_body(dcur_ref, dnext_ref, w_ref, x_ref, hs_ref, wd_ref, g_ref, b_ref, yb_ref,
                  o_ref, ob_ref, buf0_ref, buf1_ref, acc_ref, sem, *, nsteps, tb):
    i = pl.program_id(0)
    tt = x_ref.shape[0]
    hcols = x_ref.shape[1] // 2

    def row_copy(d_ref, buf, s, t, k):
        return pltpu.make_async_copy(yb_ref.at[pl.ds(d_ref[k, t], 1), :], buf.at[k, pl.ds(t, 1), :], sem.at[s])

    def drain(d_ref, buf, s):
        def body(t, c):
            for k in range(TOP_K):
                row_copy(d_ref, buf, s, t, k).wait()
            return c
        lax.fori_loop(0, tt, body, 0)

    @pl.when(i == 0)
    def _():
        def body(t, c):
            for k in range(TOP_K):
                row_copy(dcur_ref, buf0_ref, 0, t, k).start()
            return c
        lax.fori_loop(0, tt, body, 0)

    def step(cur, nxt, s):
        drain(dcur_ref, cur, s)
        acc_ref[...] = ALPHA * x_ref[...] + jnp.dot(hs_ref[...], wd_ref[...], preferred_element_type=F32)

        def body(j, c):
            r0 = pl.multiple_of(j * tb, tb)
            for off in range(tb):
                for k in range(TOP_K):
                    row_copy(dnext_ref, nxt, 1 - s, r0 + off, k).start()
            rows = pl.ds(r0, tb)
            a = acc_ref[rows, :]
            hi_acc, lo_acc = a[:, :hcols], a[:, hcols:]
            for k in range(TOP_K):
                gate = w_ref[rows, k:k + 1]
                hi, lo = _unpack_halves(cur[k, rows, :])
                hi_acc = hi_acc + gate * hi
                lo_acc = lo_acc + gate * lo
            o = _layernorm(jnp.concatenate([hi_acc, lo_acc], axis=1), g_ref[...], b_ref[...])
            o_ref[rows, :] = o
            ob_ref[rows, :] = o.astype(BF16)
            return c
        lax.fori_loop(0, tt // tb, body, 0)

        @pl.when(i == nsteps - 1)
        def _():
            drain(dnext_ref, nxt, 1 - s)

    @pl.when(i % 2 == 0)
    def _():
        step(buf0_ref, buf1_ref, 0)

    @pl.when(i % 2 == 1)
    def _():
        step(buf1_ref, buf0_ref, 1)


def _combine(dest, gates, x1, hs, wd, g, b, yb, tt):
    t, d = x1.shape
    de = hs.shape[1]
    nsteps = t // tt
    smem = functools.partial(pl.BlockSpec, memory_space=pltpu.SMEM)
    row = pl.BlockSpec((tt, d), lambda i: (i, 0))
    vec = pl.BlockSpec((1, d), lambda i: (0, 0))
    return pl.pallas_call(
        functools.partial(_combine_body, nsteps=nsteps, tb=16), grid=(nsteps,),
        in_specs=[smem((TOP_K, tt), lambda i: (0, i)),
                  smem((TOP_K, tt), lambda i: (0, jnp.minimum(i + 1, nsteps - 1))),
                  pl.BlockSpec((tt, TOP_K), lambda i: (i, 0)), row,
                  pl.BlockSpec((tt, de), lambda i: (i, 0)), pl.BlockSpec((de, d), lambda i: (0, 0)),
                  vec, vec, pl.BlockSpec(memory_space=pl.ANY)],
        out_specs=[row, row],
        out_shape=[jax.ShapeDtypeStruct((t, d), F32), jax.ShapeDtypeStruct((t, d), BF16)],
        scratch_shapes=[pltpu.VMEM((TOP_K, tt, d // 2), U32), pltpu.VMEM((TOP_K, tt, d // 2), U32),
                        pltpu.VMEM((tt, d), F32), pltpu.SemaphoreType.DMA((2,))],
        compiler_params=_cparams(("arbitrary",), 48), name="moe_combine_ln2")(
            dest, dest, gates, x1, hs, wd, g, b, yb)


def _moe_ln2(x1, x1b, xw, router_wt, router_bias, wg, wu, wd, layer, shg, shu, shd, g2, b2):
    t, d = x1.shape
    ne = router_wt.shape[0]
    rows = MOE_ROWS
    nb = -(-(t * TOP_K + ne * (rows - 1)) // rows)
    nbp = -(-nb // LANES) * LANES
    eidx, gates, rank, cnt, crow = _route(x1, router_wt, router_bias, tm=512)
    pstart, pend, bexp, nused = _meta(cnt, crow, rows, nbp)
    dest = _dest(eidx, rank, pstart, tt=2048)
    xb = _dispatch(dest, cnt.astype(I32), pend, xw, nb * rows, rows, tt=256)
    hid = _exp_up(bexp[0], nused[0, :1], xb, wg, wu, layer, rows)
    yb = _exp_down(bexp[0], nused[0, :1], hid, wd, layer, rows)
    hs = _shared_up(x1b, shg, shu, tm=1024)
    return _combine(dest, gates.T, x1, hs, shd, g2, b2, yb, tt=128)


def kernel(x_prompt, x_sample, w_in, fnet_w, s5_a_re, s5_a_im, s5_log_dt, s5_b_re, s5_b_im, s5_c_re, s5_c_im,
           s5_d, glu_w, glu_b, gn_fnet, gn_s5, w_out, ln1_g, ln1_b, router_w, router_bias, exp_w_gate, exp_w_up,
           exp_w_down, sh_w_gate, sh_w_up, sh_w_down, ln2_g, ln2_b):
    bp, sp, d = x_prompt.shape
    bs, ss, _ = x_sample.shape
    depth = w_in.shape[0]
    heads, hd = fnet_w.shape[1], fnet_w.shape[2]
    fw = heads * hd
    n2 = 128
    seqs = [(0, bp, sp // n2, n2), (bp * sp, bs, ss // n2, n2)]
    seq_lens = [sp] * bp + [ss] * bs
    consts = {"chan": _chan_const(hd)}
    for (_, _, n1, n2_) in seqs:
        consts[(n1, n2_)] = _dft_consts(n1, n2_, hd)

    x = jnp.concatenate([x_prompt.reshape(bp * sp, d), x_sample.reshape(bs * ss, d)], axis=0)
    xb = x.astype(BF16)
    for l in range(depth):
        ops = _s5_prep(s5_a_re[l], s5_a_im[l], s5_log_dt[l], s5_b_re[l], s5_b_im[l], s5_c_re[l], s5_c_im[l], s5_d[l])
        h = _mm(xb, w_in[l].astype(BF16), tm=1024, tn=512, out_dtype=BF16, name="w_in")
        f = _fnet_branch(h, fnet_w[l].astype(BF16), seqs, consts, heads, hd)
        ys = _s5_branch(h, fw, ops, seq_lens)
        sm = _glu(ys, glu_w[l].astype(BF16), glu_b[l].reshape(1, -1), tm=1024, tn=1024)
        mix = _mixnorm(f, sm, gn_fnet[l].reshape(1, -1), gn_s5[l].reshape(1, -1), tm=512)
        y1 = _mm_res(mix, w_out[l].astype(BF16), x, tm=1024, tn=512, name="w_out")
        x1, x1b, xw = _ln1(y1, ln1_g[l].reshape(1, -1), ln1_b[l].reshape(1, -1), tm=256)
        x, xb = _moe_ln2(x1, x1b, xw, router_w[l].T, router_bias[l].reshape(-1, 1), exp_w_gate, exp_w_up,
                         exp_w_down, l, sh_w_gate[l].astype(BF16), sh_w_up[l].astype(BF16),
                         sh_w_down[l].astype(BF16), ln2_g[l].reshape(1, -1), ln2_b[l].reshape(1, -1))
    return (x[:bp * sp].reshape(bp, sp, d), x[bp * sp:].reshape(bs, ss, d))
```

```python
import functools
import math

import numpy as np
import jax
import jax.numpy as jnp
from jax import lax
from jax.experimental import pallas as pl
from jax.experimental.pallas import tpu as pltpu

F32 = jnp.float32
BF16 = jnp.bfloat16
I32 = jnp.int32
U32 = jnp.uint32

DEPTH = 4
FNET_HEADS = 8
S5_P = 16
S5_N = 64
N_EXPERTS = 64
N_EXP_GROUPS = 8
EXP_PER_GROUP = N_EXPERTS // N_EXP_GROUPS
TOPK_GROUPS = 4
TOP_K = 8
ROUTE_SCALE = 2.5
ALPHA = (2 * DEPTH) ** 0.25
EPS = 1e-5

S5_L = 16
S5_PAIR = 2
MOE_ROWS = 256
V7X_VMEM_BYTES = 64 * 1024 * 1024
VMEM_CAP = V7X_VMEM_BYTES - 8 * 1024 * 1024
LANES = 128


def _cparams(sem, vmem_mb):
    return pltpu.CompilerParams(dimension_semantics=sem,
                                vmem_limit_bytes=min(int(vmem_mb * 1024 * 1024), VMEM_CAP))


def _mm_body(x_ref, w_ref, o_ref):
    o_ref[...] = jnp.dot(x_ref[...], w_ref[...], preferred_element_type=F32).astype(o_ref.dtype)


def _mm(x, w, tm, tn, out_dtype, name):
    m, k = x.shape
    n = w.shape[1]
    return pl.pallas_call(
        _mm_body, grid=(m // tm, n // tn),
        in_specs=[pl.BlockSpec((tm, k), lambda i, j: (i, 0)), pl.BlockSpec((k, tn), lambda i, j: (0, j))],
        out_specs=pl.BlockSpec((tm, tn), lambda i, j: (i, j)),
        out_shape=jax.ShapeDtypeStruct((m, n), out_dtype),
        compiler_params=_cparams(("parallel", "parallel"), 48), name=name)(x, w)


def _mm_res_body(x_ref, w_ref, r_ref, o_ref):
    acc = jnp.dot(x_ref[...], w_ref[...], preferred_element_type=F32)
    o_ref[...] = ALPHA * r_ref[...] + acc


def _mm_res(x, w, res, tm, tn, name):
    m, k = x.shape
    n = w.shape[1]
    return pl.pallas_call(
        _mm_res_body, grid=(m // tm, n // tn),
        in_specs=[pl.BlockSpec((tm, k), lambda i, j: (i, 0)), pl.BlockSpec((k, tn), lambda i, j: (0, j)),
                  pl.BlockSpec((tm, tn), lambda i, j: (i, j))],
        out_specs=pl.BlockSpec((tm, tn), lambda i, j: (i, j)),
        out_shape=jax.ShapeDtypeStruct((m, n), F32),
        compiler_params=_cparams(("parallel", "parallel"), 48), name=name)(x, w, res)


def _glu_body(y_ref, w_ref, b_ref, yt_ref, o_ref):
    g = jnp.dot(y_ref[...], w_ref[...], preferred_element_type=F32) + b_ref[...]
    o_ref[...] = (yt_ref[...].astype(F32) * jax.nn.sigmoid(g)).astype(o_ref.dtype)


def _glu(y, w, b, tm, tn):
    m, k = y.shape
    n = w.shape[1]
    return pl.pallas_call(
        _glu_body, grid=(m // tm, n // tn),
        in_specs=[pl.BlockSpec((tm, k), lambda i, j: (i, 0)), pl.BlockSpec((k, tn), lambda i, j: (0, j)),
                  pl.BlockSpec((1, tn), lambda i, j: (0, j)), pl.BlockSpec((tm, tn), lambda i, j: (i, j))],
        out_specs=pl.BlockSpec((tm, tn), lambda i, j: (i, j)),
        out_shape=jax.ShapeDtypeStruct((m, n), BF16),
        compiler_params=_cparams(("parallel", "parallel"), 32), name="s5_glu")(y, w, b, y)


def _dft_consts(n1, n2, hd):
    n = n1 * n2
    k = np.arange(n1)
    ang1 = 2 * np.pi * np.outer(k, k) / n1
    c1, s1 = np.cos(ang1), np.sin(ang1)
    w1 = np.block([[c1, s1], [-s1, c1]])
    ang_t = 2 * np.pi * np.outer(np.arange(n2), np.arange(n1)) / n
    twr = np.broadcast_to(np.cos(ang_t)[:, :, None], (n2, n1, LANES))
    twi = np.broadcast_to(-np.sin(ang_t)[:, :, None], (n2, n1, LANES))
    k2 = np.arange(n2)
    ang2 = 2 * np.pi * np.outer(k2, k2) / n2
    w2 = np.concatenate([np.cos(ang2), np.sin(ang2)], axis=1) / math.sqrt(n)
    return (jnp.asarray(w1, BF16), jnp.asarray(np.ascontiguousarray(twr), F32),
            jnp.asarray(np.ascontiguousarray(twi), F32), jnp.asarray(w2, BF16))


def _chan_const(hd):
    c = np.arange(hd)
    ang = 2 * np.pi * np.outer(c, c) / hd
    return jnp.asarray(np.concatenate([np.cos(ang), -np.sin(ang)], axis=1) / math.sqrt(hd), BF16)


def _chan_body(x_ref, w_ref, zr_ref, zi_ref, *, hd):
    r = jnp.dot(x_ref[...], w_ref[...], preferred_element_type=F32)
    zr_ref[...] = r[:, :hd].astype(BF16)
    zi_ref[...] = r[:, hd:].astype(BF16)


def _fnet_chan(h, wc, heads, hd, tm):
    t = h.shape[0]
    shp = jax.ShapeDtypeStruct((t, heads * hd), BF16)
    return pl.pallas_call(
        functools.partial(_chan_body, hd=hd), grid=(t // tm, heads),
        in_specs=[pl.BlockSpec((tm, hd), lambda i, j: (i, j)), pl.BlockSpec((hd, 2 * hd), lambda i, j: (0, 0))],
        out_specs=[pl.BlockSpec((tm, hd), lambda i, j: (i, j))] * 2,
        out_shape=[shp, shp],
        compiler_params=_cparams(("parallel", "parallel"), 32), name="fnet_chan")(h, wc)


def _stage1_body(zr_ref, zi_ref, twr_ref, twi_ref, w_ref, yr_ref, yi_ref, *, n2t, n1, reps):
    w = w_ref[...]
    for j in range(n2t):
        z = jnp.concatenate([zr_ref[j], zi_ref[j]], axis=0)
        y = jnp.dot(w, z, preferred_element_type=F32)
        yr, yi = y[:n1], y[n1:]
        twr = jnp.concatenate([twr_ref[j]] * reps, axis=1)
        twi = jnp.concatenate([twi_ref[j]] * reps, axis=1)
        yr_ref[j] = (yr * twr - yi * twi).astype(BF16)
        yi_ref[j] = (yr * twi + yi * twr).astype(BF16)


def _fnet_stage1(zr, zi, w1, twr, twi, n2t, ct):
    nseq, n2, n1, c = zr.shape
    blk = pl.BlockSpec((None, n2t, n1, ct), lambda s, a, b: (s, a, 0, b))
    twb = pl.BlockSpec((n2t, n1, LANES), lambda s, a, b: (a, 0, 0))
    shp = jax.ShapeDtypeStruct(zr.shape, BF16)
    return pl.pallas_call(
        functools.partial(_stage1_body, n2t=n2t, n1=n1, reps=ct // LANES),
        grid=(nseq, n2 // n2t, c // ct),
        in_specs=[blk, blk, twb, twb, pl.BlockSpec((2 * n1, 2 * n1), lambda s, a, b: (0, 0))],
        out_specs=[blk, blk], out_shape=[shp, shp],
        compiler_params=_cparams(("parallel", "parallel", "parallel"), 32),
        name=f"fnet_stage1_n{n1}")(zr, zi, twr, twi, w1)


def _stage2_body(yr_ref, yi_ref, w_ref, wm_ref, f_ref, *, k1t):
    w = w_ref[...]
    wm = wm_ref[0]
    for j in range(k1t):
        yy = jnp.concatenate([yr_ref[j], yi_ref[j]], axis=0)
        r = jnp.dot(w, yy, preferred_element_type=F32)
        f_ref[j] = jnp.dot(r.astype(BF16), wm, preferred_element_type=F32).astype(BF16)


def _fnet_stage2(yr, yi, w2, wmap, k1t, hd):
    nseq, n1, n2, c = yr.shape
    blk = pl.BlockSpec((None, k1t, n2, hd), lambda s, a, h: (s, a, 0, h))
    return pl.pallas_call(
        functools.partial(_stage2_body, k1t=k1t), grid=(nseq, n1 // k1t, c // hd),
        in_specs=[blk, blk, pl.BlockSpec((n2, 2 * n2), lambda s, a, h: (0, 0)),
                  pl.BlockSpec((1, hd, hd), lambda s, a, h: (h, 0, 0))],
        out_specs=blk, out_shape=jax.ShapeDtypeStruct(yr.shape, BF16),
        compiler_params=_cparams(("parallel", "parallel", "parallel"), 32),
        name=f"fnet_stage2_n{n1}")(yr, yi, w2, wmap)


def _fnet_branch(h, wmap_b, seqs, consts, heads, hd):
    c = heads * hd
    zr, zi = _fnet_chan(h, consts["chan"], heads, hd, tm=2048)
    outs = []
    for (row0, nseq, n1, n2) in seqs:
        rows = nseq * n1 * n2
        w1, twr, twi, w2 = consts[(n1, n2)]

        def to_stage1(z):
            return z[row0:row0 + rows].reshape(nseq, n1, n2, c).transpose(0, 2, 1, 3)

        yr, yi = _fnet_stage1(to_stage1(zr), to_stage1(zi), w1, twr, twi, n2t=8, ct=min(c, 1024))
        yr = yr.transpose(0, 2, 1, 3)
        yi = yi.transpose(0, 2, 1, 3)
        f = _fnet_stage2(yr, yi, w2, wmap_b, k1t=8, hd=hd)
        outs.append(f.transpose(0, 2, 1, 3).reshape(rows, c))
    return jnp.concatenate(outs, axis=0)


def _s5_prep_body(are_ref, aim_ref, ldt_ref, brt_ref, bit_ref, cr_ref, ci_ref, crt_ref, cit_ref, dd_ref,
                  m_ref, p_ref, q_ref, lam_ref):
    L, P = S5_L, S5_P
    R = L * P
    lane = lax.broadcasted_iota(I32, (R, LANES), 1)
    member = lane // S5_N
    slot = lane // P
    row = lax.broadcasted_iota(I32, (L, LANES), 0).astype(F32)

    def powers(ar, ai, dt, tau):
        e = jnp.exp(tau * (ar * dt))
        ang = tau * (ai * dt)
        return e * jnp.cos(ang), e * jnp.sin(ang)

    def expand(x):
        return jnp.concatenate([jnp.broadcast_to(x[t:t + 1], (P, LANES)) for t in range(L)], axis=0)

    def tile(x):
        return jnp.concatenate([x] * L, axis=0)

    def cmul(ar_, ai_, br_, bi_):
        return ar_ * br_ - ai_ * bi_, ar_ * bi_ + ai_ * br_

    gs, qs, kts, lams = [], [], [], []
    for d in (0, 1):
        ar, ai = are_ref[d], aim_ref[d]
        dt = jnp.exp(ldt_ref[d])
        one = jnp.ones((1, LANES), F32)
        lr, li = powers(ar, ai, dt, one)
        den = ar * ar + ai * ai
        xr, xi = lr - 1.0, li
        cfr = (xr * ar + xi * ai) / den
        cfi = (xi * ar - xr * ai) / den
        bbr, bbi = cmul(cfr, cfi, brt_ref[d], bit_ref[d])
        tau_g = (L - 1.0 - row) if d == 0 else row
        tau_q = (row + 1.0) if d == 0 else (L - row)
        pr, pi = powers(ar, ai, dt, tau_g)
        gr, gi = cmul(expand(pr), expand(pi), tile(bbr), tile(bbi))
        qr, qi = powers(ar, ai, dt, tau_q)
        er, ei = cmul(tile(cr_ref[d]), tile(ci_ref[d]), expand(qr), expand(qi))
        gs.append((gr, gi))
        qs.append((er, -ei))
        lams.extend(powers(ar, ai, dt, float(L) * one))
        rhs = jnp.concatenate([crt_ref[d], cit_ref[d]], axis=0)
        per_member = []
        for g in range(S5_PAIR):
            msk = member == g
            lhs = jnp.concatenate([jnp.where(msk, gr, 0.0), jnp.where(msk, -gi, 0.0)], axis=1)
            per_member.append(jnp.dot(lhs, rhs, preferred_element_type=F32, precision=lax.Precision.HIGHEST))
        kts.append(per_member)

    m_ref[...] = jnp.zeros(m_ref.shape, m_ref.dtype)
    for g in range(S5_PAIR):
        ktf, ktb = kts[0][g], kts[1][g]
        v = jnp.concatenate([ktf[:R - P], ktf[R - P:] + ktb[:P] + dd_ref[g], ktb[P:]], axis=0)
        for hh in range(R // LANES):
            acc = jnp.zeros((R, LANES), F32)
            for il in range(LANES // P):
                i = hh * (LANES // P) + il
                acc = jnp.where(slot == il, v[(L - 1 - i) * P:(L - 1 - i) * P + R], acc)
            m_ref[g * R:(g + 1) * R, g * R + hh * LANES:g * R + (hh + 1) * LANES] = acc.astype(m_ref.dtype)
        msk = member == g
        comps_p = [gs[0][0], gs[0][1], gs[1][0], gs[1][1]]
        comps_q = [qs[0][0], qs[0][1], qs[1][0], qs[1][1]]
        for c in range(4):
            p_ref[g * R:(g + 1) * R, c * LANES:(c + 1) * LANES] = jnp.where(msk, comps_p[c], 0.0).astype(p_ref.dtype)
            q_ref[g * R:(g + 1) * R, c * LANES:(c + 1) * LANES] = jnp.where(msk, comps_q[c], 0.0).astype(q_ref.dtype)
    lam_ref[...] = jnp.concatenate(lams, axis=1)


def _s5_prep(a_re, a_im, log_dt, b_re, b_im, c_re, c_im, d_skip):
    g, n, p = b_re.shape[1], b_re.shape[2], b_re.shape[3]
    sg = g // S5_PAIR
    assert n == S5_N and p == S5_P and S5_PAIR * n == LANES
    lanes4 = lambda x: x.reshape(2, sg, 1, LANES).astype(F32)
    are, aim = lanes4(a_re), lanes4(a_im)
    ldt = lanes4(jnp.repeat(log_dt, n, axis=1))
    bt = lambda x: x.reshape(2, sg, S5_PAIR, n, p).transpose(0, 1, 4, 2, 3).reshape(2, sg, p, LANES).astype(F32)
    cc = lambda x: x.reshape(2, sg, S5_PAIR, p, n).transpose(0, 1, 3, 2, 4).reshape(2, sg, p, LANES).astype(F32)
    ct = lambda x: jnp.tile(x.reshape(2, sg, S5_PAIR, p, n).transpose(0, 1, 2, 4, 3).reshape(2, sg, LANES, p),
                            (1, 1, 1, LANES // p)).astype(F32)
    eye = jnp.eye(p, dtype=F32)
    dd = d_skip.reshape(sg, S5_PAIR, 1, p).astype(F32) * eye[None, None]
    dd = jnp.tile(dd, (1, 1, 1, LANES // p))
    r2 = S5_PAIR * S5_L * p
    v1 = pl.BlockSpec((2, None, 1, LANES), lambda s: (0, s, 0, 0))
    vp = pl.BlockSpec((2, None, p, LANES), lambda s: (0, s, 0, 0))
    vt = pl.BlockSpec((2, None, LANES, LANES), lambda s: (0, s, 0, 0))
    mat = pl.BlockSpec((None, r2, r2), lambda s: (s, 0, 0))
    mshape = jax.ShapeDtypeStruct((sg, r2, r2), BF16)
    return pl.pallas_call(
        _s5_prep_body, grid=(sg,),
        in_specs=[v1, v1, v1, vp, vp, vp, vp, vt, vt, pl.BlockSpec((None, S5_PAIR, p, LANES), lambda s: (s, 0, 0, 0))],
        out_specs=[mat, mat, mat, pl.BlockSpec((None, 1, 4 * LANES), lambda s: (s, 0, 0))],
        out_shape=[mshape, mshape, mshape, jax.ShapeDtypeStruct((sg, 1, 4 * LANES), F32)],
        compiler_params=_cparams(("parallel",), 32), name="s5_prep")(
            are, aim, ldt, bt(b_re), bt(b_im), cc(c_re), cc(c_im), ct(c_re), ct(c_im), dd)


def _s5_statein(u, pm, nct):
    sg, nc, r2 = u.shape
    return pl.pallas_call(
        _mm_body, grid=(sg, nc // nct),
        in_specs=[pl.BlockSpec((None, nct, r2), lambda s, c: (s, c, 0)),
                  pl.BlockSpec((None, r2, pm.shape[2]), lambda s, c: (s, 0, 0))],
        out_specs=pl.BlockSpec((None, nct, pm.shape[2]), lambda s, c: (s, c, 0)),
        out_shape=jax.ShapeDtypeStruct((sg, nc, pm.shape[2]), F32),
        compiler_params=_cparams(("parallel", "parallel"), 32), name="s5_statein")(u, pm)


def _s5_scan_body(fk_ref, bk_ref, rs_ref, pf_ref, pb_ref, lam_ref, sf_ref, sb_ref, st_ref, *, nct):
    k = pl.program_id(0)

    @pl.when(rs_ref[k] == 1)
    def _():
        st_ref[...] = jnp.zeros(st_ref.shape, F32)

    lam = lam_ref[...]
    lfr, lfi = lam[:, 0:LANES], lam[:, LANES:2 * LANES]
    lbr, lbi = lam[:, 2 * LANES:3 * LANES], lam[:, 3 * LANES:]

    def step(i, carry):
        fr, fi, br, bi = carry
        cb = nct - 1 - i
        sf_ref[i] = jnp.concatenate([fr, fi], axis=1).astype(BF16)
        sb_ref[cb] = jnp.concatenate([br, bi], axis=1).astype(BF16)
        pf = pf_ref[i]
        pb = pb_ref[cb]
        fr2 = lfr * fr - lfi * fi + pf[:, :LANES]
        fi2 = lfr * fi + lfi * fr + pf[:, LANES:]
        br2 = lbr * br - lbi * bi + pb[:, :LANES]
        bi2 = lbr * bi + lbi * br + pb[:, LANES:]
        return fr2, fi2, br2, bi2

    out = lax.fori_loop(0, nct, step, (st_ref[0], st_ref[1], st_ref[2], st_ref[3]))
    for j in range(4):
        st_ref[j] = out[j]


def _s5_scan(pin, lam, seq_chunks, nct):
    nc, sg, _ = pin.shape
    fk, bk, rs = [], [], []
    base = 0
    for n_chunks in seq_chunks:
        nb = n_chunks // nct
        for j in range(nb):
            fk.append(base + j)
            bk.append(base + nb - 1 - j)
            rs.append(1 if j == 0 else 0)
        base += nb
    nsteps = len(fk)
    to = lambda v: jnp.asarray(np.asarray(v, np.int32))
    blk_f = pl.BlockSpec((nct, sg, 2 * LANES), lambda k, fk_, bk_, rs_: (fk_[k], 0, 0))
    blk_b = pl.BlockSpec((nct, sg, 2 * LANES), lambda k, fk_, bk_, rs_: (bk_[k], 0, 1))
    out_f = pl.BlockSpec((nct, sg, 2 * LANES), lambda k, fk_, bk_, rs_: (fk_[k], 0, 0))
    out_b = pl.BlockSpec((nct, sg, 2 * LANES), lambda k, fk_, bk_, rs_: (bk_[k], 0, 0))
    shp = jax.ShapeDtypeStruct((nc, sg, 2 * LANES), BF16)
    return pl.pallas_call(
        functools.partial(_s5_scan_body, nct=nct),
        grid_spec=pltpu.PrefetchScalarGridSpec(
            num_scalar_prefetch=3, grid=(nsteps,),
            in_specs=[blk_f, blk_b, pl.BlockSpec((sg, 4 * LANES), lambda k, *_: (0, 0))],
            out_specs=[out_f, out_b],
            scratch_shapes=[pltpu.VMEM((4, sg, LANES), F32)]),
        out_shape=[shp, shp],
        compiler_params=_cparams(("arbitrary",), 40), name="s5_scan")(to(fk), to(bk), to(rs), pin, pin, lam)


def _s5_out_body(u_ref, m_ref, q_ref, sf_ref, sb_ref, y_ref):
    y = jnp.dot(u_ref[...], m_ref[...], preferred_element_type=F32)
    s = jnp.concatenate([sf_ref[...], sb_ref[...]], axis=1)
    y = y + lax.dot_general(s, q_ref[...], (((1,), (1,)), ((), ())), preferred_element_type=F32)
    y_ref[...] = jax.nn.gelu(y).astype(y_ref.dtype)


def _s5_out(u, mm, qt, sf, sb, nct):
    sg, nc, r2 = u.shape
    ublk = pl.BlockSpec((None, nct, r2), lambda s, c: (s, c, 0))
    sblk = pl.BlockSpec((None, nct, 2 * LANES), lambda s, c: (s, c, 0))
    mblk = pl.BlockSpec((None, r2, r2), lambda s, c: (s, 0, 0))
    return pl.pallas_call(
        _s5_out_body, grid=(sg, nc // nct),
        in_specs=[ublk, mblk, mblk, sblk, sblk], out_specs=ublk,
        out_shape=jax.ShapeDtypeStruct(u.shape, BF16),
        compiler_params=_cparams(("parallel", "parallel"), 32), name="s5_out")(u, mm, qt, sf, sb)


def _s5_perm_const():
    half = S5_L // 2
    ppl = LANES // (S5_PAIR * S5_P)
    n = half * LANES
    e = np.zeros((n, n), np.float32)
    for sl in range(half):
        for pr in range(ppl):
            for g in range(S5_PAIR):
                for q in range(S5_P):
                    src = sl * LANES + pr * S5_PAIR * S5_P + g * S5_P + q
                    dst = (pr * S5_PAIR + g) * LANES + sl * S5_P + q
                    e[src, dst] = 1.0
    return e


def _s5_flatten_body(x_ref, e_ref, u_ref, scr_ref, *, nct):
    half = S5_L // 2
    r = S5_L * S5_P
    scr_ref[...] = x_ref[...].astype(F32).reshape(nct * S5_L, LANES)
    e = e_ref[...]
    for hh in range(2):
        pieces = [scr_ref[pl.ds(hh * half + sl, nct, stride=S5_L), :].astype(BF16) for sl in range(half)]
        res = jnp.dot(jnp.concatenate(pieces, axis=1), e, preferred_element_type=F32).astype(BF16)
        for j in range(LANES // S5_P):
            pr, g = divmod(j, S5_PAIR)
            u_ref[pr, :, g * r + hh * LANES:g * r + (hh + 1) * LANES] = res[:, j * LANES:(j + 1) * LANES]


def _s5_flatten(h, col0, w, nct):
    t = h.shape[0]
    nc = t // S5_L
    ppl = LANES // (S5_PAIR * S5_P)
    sg = w // (S5_PAIR * S5_P)
    r2 = S5_PAIR * S5_L * S5_P
    e = jnp.asarray(_s5_perm_const(), BF16)
    return pl.pallas_call(
        functools.partial(_s5_flatten_body, nct=nct), grid=(nc // nct, w // LANES),
        in_specs=[pl.BlockSpec((nct, S5_L, LANES), lambda c, l: (c, 0, col0 // LANES + l)),
                  pl.BlockSpec(e.shape, lambda c, l: (0, 0))],
        out_specs=pl.BlockSpec((ppl, nct, r2), lambda c, l: (l, c, 0)),
        out_shape=jax.ShapeDtypeStruct((sg, nc, r2), BF16),
        scratch_shapes=[pltpu.VMEM((nct * S5_L, LANES), F32)],
        compiler_params=_cparams(("parallel", "parallel"), 32), name="s5_flatten")(
            h.reshape(nc, S5_L, h.shape[1]), e)


def _s5_unflatten_body(y_ref, et_ref, o_ref, scr_ref, *, nct):
    half = S5_L // 2
    r = S5_L * S5_P
    et = et_ref[...]
    for hh in range(2):
        parts = []
        for j in range(LANES // S5_P):
            pr, g = divmod(j, S5_PAIR)
            parts.append(y_ref[pr, :, g * r + hh * LANES:g * r + (hh + 1) * LANES])
        res = jnp.dot(jnp.concatenate(parts, axis=1), et, preferred_element_type=F32)
        for sl in range(half):
            scr_ref[pl.ds(hh * half + sl, nct, stride=S5_L), :] = res[:, sl * LANES:(sl + 1) * LANES]
    o_ref[...] = scr_ref[...].reshape(nct, S5_L, LANES).astype(o_ref.dtype)


def _s5_unflatten(yg, nct):
    sg, nc, r2 = yg.shape
    ppl = LANES // (S5_PAIR * S5_P)
    w = sg * S5_PAIR * S5_P
    et = jnp.asarray(_s5_perm_const().T, BF16)
    out = pl.pallas_call(
        functools.partial(_s5_unflatten_body, nct=nct), grid=(nc // nct, w // LANES),
        in_specs=[pl.BlockSpec((ppl, nct, r2), lambda c, l: (l, c, 0)), pl.BlockSpec(et.shape, lambda c, l: (0, 0))],
        out_specs=pl.BlockSpec((nct, S5_L, LANES), lambda c, l: (c, 0, l)),
        out_shape=jax.ShapeDtypeStruct((nc, S5_L, w), BF16),
        scratch_shapes=[pltpu.VMEM((nct * S5_L, LANES), F32)],
        compiler_params=_cparams(("parallel", "parallel"), 32), name="s5_unflatten")(yg, et)
    return out.reshape(nc * S5_L, w)


def _s5_branch(h, col0, ops, seq_lens):
    mm, pm, qt, lam = ops
    t = h.shape[0]
    sg = mm.shape[0]
    w = sg * S5_PAIR * S5_P
    nc = t // S5_L
    u = _s5_flatten(h, col0, w, nct=256)
    pin = _s5_statein(u, pm, nct=nc // 2).transpose(1, 0, 2)
    sf, sb = _s5_scan(pin, lam.reshape(sg, 4 * LANES), [s // S5_L for s in seq_lens], nct=64)
    yg = _s5_out(u, mm, qt, sf.transpose(1, 0, 2), sb.transpose(1, 0, 2), nct=nc // 2)
    return _s5_unflatten(yg, nct=256)


def _mixnorm_body(f_ref, s_ref, gf_ref, gs_ref, o_ref, *, w):
    def rn(x, g):
        xf = x.astype(F32)
        return (xf * lax.rsqrt(jnp.mean(xf * xf, axis=-1, keepdims=True) + EPS) * g).astype(o_ref.dtype)
    o_ref[:, :w] = rn(f_ref[...], gf_ref[...])
    o_ref[:, w:] = rn(s_ref[...], gs_ref[...])


def _mixnorm(f, s, gf, gs, tm):
    t, w = f.shape
    return pl.pallas_call(
        functools.partial(_mixnorm_body, w=w), grid=(t // tm,),
        in_specs=[pl.BlockSpec((tm, w), lambda i: (i, 0)), pl.BlockSpec((tm, s.shape[1]), lambda i: (i, 0)),
                  pl.BlockSpec((1, w), lambda i: (0, 0)), pl.BlockSpec((1, s.shape[1]), lambda i: (0, 0))],
        out_specs=pl.BlockSpec((tm, w + s.shape[1]), lambda i: (i, 0)),
        out_shape=jax.ShapeDtypeStruct((t, w + s.shape[1]), BF16),
        compiler_params=_cparams(("parallel",), 32), name="mix_norm")(f, s, gf, gs)


def _layernorm(y, g, b):
    mu = jnp.mean(y, axis=-1, keepdims=True)
    yc = y - mu
    var = jnp.mean(yc * yc, axis=-1, keepdims=True)
    return yc * lax.rsqrt(var + EPS) * g + b


def _pack_halves(ob):
    hcols = ob.shape[1] // 2
    hi = lax.bitcast_convert_type(ob[:, :hcols].astype(F32), U32)
    lo = lax.bitcast_convert_type(ob[:, hcols:].astype(F32), U32)
    return hi | (lo >> 16)


def _unpack_halves(words):
    hi = lax.bitcast_convert_type(words & jnp.uint32(0xFFFF0000), F32)
    lo = lax.bitcast_convert_type(words << 16, F32)
    return hi, lo


def _ln1_body(y_ref, g_ref, b_ref, x_ref, xb_ref, xw_ref):
    o = _layernorm(y_ref[...], g_ref[...], b_ref[...])
    x_ref[...] = o
    ob = o.astype(BF16)
    xb_ref[...] = ob
    xw_ref[...] = _pack_halves(ob)


def _ln1(y, g, b, tm):
    t, d = y.shape
    row = pl.BlockSpec((tm, d), lambda i: (i, 0))
    vec = pl.BlockSpec((1, d), lambda i: (0, 0))
    return pl.pallas_call(
        _ln1_body, grid=(t // tm,), in_specs=[row, vec, vec],
        out_specs=[row, row, pl.BlockSpec((tm, d // 2), lambda i: (i, 0))],
        out_shape=[jax.ShapeDtypeStruct((t, d), F32), jax.ShapeDtypeStruct((t, d), BF16),
                   jax.ShapeDtypeStruct((t, d // 2), U32)],
        compiler_params=_cparams(("parallel",), 48), name="ln1")(y, g, b)


def _route_body(x_ref, wt_ref, bias_ref, tri_ref, e_ref, w_ref, r_ref, cnt_ref, crow_ref, carry_ref, rowc_ref):
    i = pl.program_id(0)

    @pl.when(i == 0)
    def _():
        carry_ref[...] = jnp.zeros(carry_ref.shape, F32)
        rowc_ref[...] = jnp.zeros(rowc_ref.shape, F32)

    ne = wt_ref.shape[0]
    tm = x_ref.shape[0]
    x = x_ref[...]
    xh = x.astype(BF16)
    xl = (x - xh.astype(F32)).astype(BF16)
    wt = wt_ref[...]
    wh = wt.astype(BF16)
    wl = (wt - wh.astype(F32)).astype(BF16)
    dn = (((1,), (1,)), ((), ()))
    dg = lambda a, b: lax.dot_general(a, b, dn, preferred_element_type=F32)
    logits = dg(wh, xh) + (dg(wh, xl) + dg(wl, xh))
    scores = jax.nn.sigmoid(logits)
    sel = scores + bias_ref[...]
    neg = jnp.float32(-jnp.inf)

    sub = lax.broadcasted_iota(I32, (EXP_PER_GROUP, tm), 0)
    gscore = []
    for g in range(N_EXP_GROUPS):
        blk = sel[g * EXP_PER_GROUP:(g + 1) * EXP_PER_GROUP]
        m1 = jnp.max(blk, axis=0, keepdims=True)
        i1 = jnp.min(jnp.where(blk == m1, sub, EXP_PER_GROUP), axis=0, keepdims=True)
        m2 = jnp.max(jnp.where(sub == i1, neg, blk), axis=0, keepdims=True)
        gscore.append(m1 + m2)
    keep = []
    for g in range(N_EXP_GROUPS):
        beat = jnp.zeros((1, tm), I32)
        for g2 in range(N_EXP_GROUPS):
            if g2 == g:
                continue
            wins = (gscore[g2] > gscore[g]) | ((gscore[g2] == gscore[g]) & (g2 < g))
            beat = beat + wins.astype(I32)
        keep.append(beat < TOPK_GROUPS)
    cand = jnp.concatenate(
        [jnp.where(keep[g], sel[g * EXP_PER_GROUP:(g + 1) * EXP_PER_GROUP], neg) for g in range(N_EXP_GROUPS)],
        axis=0)
    eid = lax.broadcasted_iota(I32, (ne, tm), 0)
    onehot = jnp.zeros((ne, tm), F32)
    idxs, wts = [], []
    for _ in range(TOP_K):
        m = jnp.max(cand, axis=0, keepdims=True)
        idx = jnp.min(jnp.where(cand == m, eid, ne), axis=0, keepdims=True)
        hit = eid == idx
        idxs.append(idx)
        wts.append(jnp.sum(jnp.where(hit, scores, 0.0), axis=0, keepdims=True))
        onehot = onehot + hit.astype(F32)
        cand = jnp.where(hit, neg, cand)
    wsum = wts[0]
    for k in range(1, TOP_K):
        wsum = wsum + wts[k]
    oh_b = onehot.astype(BF16)
    pre = jnp.dot(oh_b, tri_ref[...], preferred_element_type=F32) + carry_ref[:, 0:1]
    for k in range(TOP_K):
        e_ref[k:k + 1, :] = idxs[k]
        w_ref[k:k + 1, :] = wts[k] / wsum * ROUTE_SCALE
        r_ref[k:k + 1, :] = jnp.sum(jnp.where(eid == idxs[k], pre, 0.0), axis=0, keepdims=True).astype(I32)
    carry_ref[...] = carry_ref[...] + jnp.sum(onehot, axis=1, keepdims=True)
    rowc_ref[...] = rowc_ref[...] + dg(jnp.ones((8, tm), BF16), oh_b)
    cnt_ref[...] = carry_ref[...]
    crow_ref[...] = rowc_ref[...]


def _route(x, wt, bias, tm):
    t, d = x.shape
    ne = wt.shape[0]
    tri = jnp.asarray(np.triu(np.ones((tm, tm), np.float32), k=1), BF16)
    kt = pl.BlockSpec((TOP_K, tm), lambda i: (0, i))
    return pl.pallas_call(
        _route_body, grid=(t // tm,),
        in_specs=[pl.BlockSpec((tm, d), lambda i: (i, 0)), pl.BlockSpec((ne, d), lambda i: (0, 0)),
                  pl.BlockSpec((ne, 1), lambda i: (0, 0)), pl.BlockSpec((tm, tm), lambda i: (0, 0))],
        out_specs=[kt, kt, kt, pl.BlockSpec((ne, LANES), lambda i: (0, 0)), pl.BlockSpec((8, ne), lambda i: (0, 0))],
        out_shape=[jax.ShapeDtypeStruct((TOP_K, t), I32), jax.ShapeDtypeStruct((TOP_K, t), F32),
                   jax.ShapeDtypeStruct((TOP_K, t), I32), jax.ShapeDtypeStruct((ne, LANES), F32),
                   jax.ShapeDtypeStruct((8, ne), F32)],
        scratch_shapes=[pltpu.VMEM((ne, LANES), F32), pltpu.VMEM((8, ne), F32)],
        compiler_params=_cparams(("arbitrary",), 40), name="moe_route")(x, wt, bias, tri)


def _meta_body(cnt_ref, crow_ref, ps_ref, pe_ref, be_ref, nu_ref, *, rows, nbp):
    ne = cnt_ref.shape[0]
    pad = lambda c: jnp.floor((c + (rows - 1)) / rows) * rows
    prow = pad(crow_ref[0:1, :])
    pcol = pad(cnt_ref[:, 0:1])
    er = lax.broadcasted_iota(I32, (ne, ne), 0)
    ec = lax.broadcasted_iota(I32, (ne, ne), 1)
    pstart = jnp.sum(jnp.where(ec < er, prow, 0.0), axis=1, keepdims=True)
    pend = pstart + pcol
    ps_ref[...] = jnp.broadcast_to(pstart, ps_ref.shape).astype(I32)
    pe_ref[...] = jnp.broadcast_to(pend, pe_ref.shape).astype(I32)
    bstart = lax.broadcasted_iota(I32, (ne, nbp), 1).astype(F32) * rows
    be = jnp.sum((pend <= bstart).astype(F32), axis=0, keepdims=True)
    be_ref[...] = jnp.broadcast_to(jnp.minimum(be, ne - 1.0), be_ref.shape).astype(I32)
    total = jnp.sum(prow, axis=1, keepdims=True) / rows
    nu_ref[...] = jnp.broadcast_to(total, nu_ref.shape).astype(I32)


def _meta(cnt, crow, rows, nbp):
    ne = cnt.shape[0]
    return pl.pallas_call(
        functools.partial(_meta_body, rows=rows, nbp=nbp),
        out_shape=[jax.ShapeDtypeStruct((ne, LANES), I32), jax.ShapeDtypeStruct((ne, LANES), I32),
                   jax.ShapeDtypeStruct((8, nbp), I32), jax.ShapeDtypeStruct((8, LANES), I32)],
        name="moe_meta")(cnt, crow)


def _dest_body(e_ref, r_ref, ps_ref, d_ref):
    ne = ps_ref.shape[0]
    tt = e_ref.shape[1]
    eid = lax.broadcasted_iota(I32, (ne, tt), 0)
    ps = ps_ref[:, 0:1].astype(F32)
    for k in range(TOP_K):
        start = jnp.sum(jnp.where(eid == e_ref[k:k + 1, :], ps, 0.0), axis=0, keepdims=True)
        d_ref[k:k + 1, :] = start.astype(I32) + r_ref[k:k + 1, :]


def _dest(e, r, ps, tt):
    t = e.shape[1]
    kt = pl.BlockSpec((TOP_K, tt), lambda i: (0, i))
    return pl.pallas_call(
        _dest_body, grid=(t // tt,),
        in_specs=[kt, kt, pl.BlockSpec(ps.shape, lambda i: (0, 0))], out_specs=kt,
        out_shape=jax.ShapeDtypeStruct((TOP_K, t), I32),
        compiler_params=_cparams(("parallel",), 16), name="moe_dest")(e, r, ps)


def _dispatch_body(dest_ref, cnt_ref, pe_ref, xw_ref, xb_ref, zero_ref, sem, zsem, *, rows):
    i = pl.program_id(0)
    tt = xw_ref.shape[0]
    ne = pe_ref.shape[0]

    def zcopy(e):
        start = pl.multiple_of(pe_ref[e, 0] - rows, rows)
        return pltpu.make_async_copy(zero_ref, xb_ref.at[pl.ds(start, rows), :], zsem)

    @pl.when(i == 0)
    def _():
        zero_ref[...] = jnp.zeros(zero_ref.shape, zero_ref.dtype)

        def start(e, c):
            @pl.when(cnt_ref[e, 0] > 0)
            def _():
                zcopy(e).start()
            return c

        def wait(e, c):
            @pl.when(cnt_ref[e, 0] > 0)
            def _():
                zcopy(e).wait()
            return c

        lax.fori_loop(0, ne, start, 0)
        lax.fori_loop(0, ne, wait, 0)

    def row_copy(t, k):
        return pltpu.make_async_copy(xw_ref.at[pl.ds(t, 1), :], xb_ref.at[pl.ds(dest_ref[k, t], 1), :], sem)

    def issue(t, c):
        for k in range(TOP_K):
            row_copy(t, k).start()
        return c

    def drain(t, c):
        for k in range(TOP_K):
            row_copy(t, k).wait()
        return c

    lax.fori_loop(0, tt, issue, 0)
    lax.fori_loop(0, tt, drain, 0)


def _dispatch(dest, cnt_i, pend, xw, n_slots, rows, tt):
    t, hw = xw.shape
    smem = functools.partial(pl.BlockSpec, memory_space=pltpu.SMEM)
    return pl.pallas_call(
        functools.partial(_dispatch_body, rows=rows), grid=(t // tt,),
        in_specs=[smem((TOP_K, tt), lambda i: (0, i)), smem(cnt_i.shape, lambda i: (0, 0)),
                  smem(pend.shape, lambda i: (0, 0)), pl.BlockSpec((tt, hw), lambda i: (i, 0))],
        out_specs=pl.BlockSpec(memory_space=pl.ANY),
        out_shape=jax.ShapeDtypeStruct((n_slots, hw), U32),
        scratch_shapes=[pltpu.VMEM((rows, hw), U32), pltpu.SemaphoreType.DMA(()), pltpu.SemaphoreType.DMA(())],
        compiler_params=_cparams(("arbitrary",), 24), name="moe_dispatch")(dest, cnt_i, pend, xw)


def _silu_mul(g, u):
    return jax.nn.silu(g) * u


def _expert_weights_step(i, be_ref, nu_ref, layer, hbm_refs, stage_refs, bf_refs, sem, par_ref):
    nu = nu_ref[0]
    e = be_ref[i]
    fresh = (i == 0) | (e != be_ref[jnp.maximum(i - 1, 0)])

    def copies(ex, s):
        return [pltpu.make_async_copy(h.at[layer, ex], st.at[s], sem.at[n, s])
                for n, (h, st) in enumerate(zip(hbm_refs, stage_refs))]

    @pl.when((i < nu) & fresh)
    def _():
        @pl.when(i == 0)
        def _():
            par_ref[0] = 0
            for c in copies(e, 0):
                c.start()

        s = par_ref[0]
        for c in copies(e, s):
            c.wait()
        nxt = lax.while_loop(lambda j: (j < nu) & (be_ref[jnp.minimum(j, nu - 1)] == e), lambda j: j + 1, i + 1)

        @pl.when(nxt < nu)
        def _():
            for c in copies(be_ref[jnp.minimum(nxt, nu - 1)], 1 - s):
                c.start()

        for st, bf in zip(stage_refs, bf_refs):
            bf[...] = st[s].astype(BF16)
        par_ref[0] = 1 - s


def _exp_up_body(be_ref, nu_ref, xb_ref, wg_ref, wu_ref, h_ref, wgf_ref, wuf_ref, wgb_ref, wub_ref, sem, par_ref,
                 *, layer):
    i = pl.program_id(0)
    valid = i < nu_ref[0]
    _expert_weights_step(i, be_ref, nu_ref, layer, (wg_ref, wu_ref), (wgf_ref, wuf_ref), (wgb_ref, wub_ref),
                         sem, par_ref)

    @pl.when(valid)
    def _():
        hcols = xb_ref.shape[1]
        hi, lo = _unpack_halves(xb_ref[...])
        xa, xc = hi.astype(BF16), lo.astype(BF16)
        dot = lambda a, w: jnp.dot(a, w, preferred_element_type=F32)
        g = dot(xa, wgb_ref[:hcols]) + dot(xc, wgb_ref[hcols:])
        u = dot(xa, wub_ref[:hcols]) + dot(xc, wub_ref[hcols:])
        h_ref[...] = _silu_mul(g, u).astype(h_ref.dtype)

    @pl.when(jnp.logical_not(valid))
    def _():
        h_ref[...] = jnp.zeros(h_ref.shape, h_ref.dtype)


def _exp_up(bexp, nused, xb, wg, wu, layer, rows):
    n_slots, hw = xb.shape
    _, ne, d, de = wg.shape
    nb = n_slots // rows
    last = lambda i, be, nu: jnp.minimum(i, nu[0] - 1)
    wspec = pl.BlockSpec(memory_space=pl.ANY)
    return pl.pallas_call(
        functools.partial(_exp_up_body, layer=layer),
        grid_spec=pltpu.PrefetchScalarGridSpec(
            num_scalar_prefetch=2, grid=(nb,),
            in_specs=[pl.BlockSpec((rows, hw), lambda i, be, nu: (last(i, be, nu), 0)), wspec, wspec],
            out_specs=pl.BlockSpec((rows, de), lambda i, be, nu: (i, 0)),
            scratch_shapes=[pltpu.VMEM((2, d, de), F32), pltpu.VMEM((2, d, de), F32),
                            pltpu.VMEM((d, de), BF16), pltpu.VMEM((d, de), BF16),
                            pltpu.SemaphoreType.DMA((2, 2)), pltpu.SMEM((1,), I32)]),
        out_shape=jax.ShapeDtypeStruct((n_slots, de), BF16),
        compiler_params=_cparams(("arbitrary",), 56), name="moe_expert_up")(bexp, nused, xb, wg, wu)


def _exp_down_body(be_ref, nu_ref, h_ref, wd_ref, y_ref, wdf_ref, wdb_ref, sem, par_ref, *, layer):
    i = pl.program_id(0)
    valid = i < nu_ref[0]
    _expert_weights_step(i, be_ref, nu_ref, layer, (wd_ref,), (wdf_ref,), (wdb_ref,), sem, par_ref)

    @pl.when(valid)
    def _():
        y = jnp.dot(h_ref[...], wdb_ref[...], preferred_element_type=F32)
        y_ref[...] = _pack_halves(y.astype(BF16))

    @pl.when(jnp.logical_not(valid))
    def _():
        y_ref[...] = jnp.zeros(y_ref.shape, y_ref.dtype)


def _exp_down(bexp, nused, h, wd, layer, rows):
    n_slots, de = h.shape
    d = wd.shape[3]
    nb = n_slots // rows
    last = lambda i, be, nu: jnp.minimum(i, nu[0] - 1)
    return pl.pallas_call(
        functools.partial(_exp_down_body, layer=layer),
        grid_spec=pltpu.PrefetchScalarGridSpec(
            num_scalar_prefetch=2, grid=(nb,),
            in_specs=[pl.BlockSpec((rows, de), lambda i, be, nu: (last(i, be, nu), 0)),
                      pl.BlockSpec(memory_space=pl.ANY)],
            out_specs=pl.BlockSpec((rows, d // 2), lambda i, be, nu: (i, 0)),
            scratch_shapes=[pltpu.VMEM((2, de, d), F32), pltpu.VMEM((de, d), BF16),
                            pltpu.SemaphoreType.DMA((1, 2)), pltpu.SMEM((1,), I32)]),
        out_shape=jax.ShapeDtypeStruct((n_slots, d // 2), U32),
        compiler_params=_cparams(("arbitrary",), 40), name="moe_expert_down")(bexp, nused, h, wd)


def _shared_up_body(x_ref, wg_ref, wu_ref, o_ref):
    x = x_ref[...]
    g = jnp.dot(x, wg_ref[...], preferred_element_type=F32)
    u = jnp.dot(x, wu_ref[...], preferred_element_type=F32)
    o_ref[...] = _silu_mul(g, u).astype(o_ref.dtype)


def _shared_up(x, wg, wu, tm):
    t, d = x.shape
    de = wg.shape[1]
    wspec = pl.BlockSpec((d, de), lambda i: (0, 0))
    return pl.pallas_call(
        _shared_up_body, grid=(t // tm,),
        in_specs=[pl.BlockSpec((tm, d), lambda i: (i, 0)), wspec, wspec],
        out_specs=pl.BlockSpec((tm, de), lambda i: (i, 0)),
        out_shape=jax.ShapeDtypeStruct((t, de), BF16),
        compiler_params=_cparams(("parallel",), 48), name="moe_shared_up")(x, wg, wu)


def _combine_body(dcur_ref, dnext_ref, w_ref, x_ref, hs_ref, wd_ref, g_ref, b_ref, yb_ref,
                  o_ref, ob_ref, buf0_ref, buf1_ref, acc_ref, sem, *, nsteps, tb):
    i = pl.program_id(0)
    tt = x_ref.shape[0]
    hcols = x_ref.shape[1] // 2

    def row_copy(d_ref, buf, s, t, k):
        return pltpu.make_async_copy(yb_ref.at[pl.ds(d_ref[k, t], 1), :], buf.at[k, pl.ds(t, 1), :], sem.at[s])

    def drain(d_ref, buf, s):
        def body(t, c):
            for k in range(TOP_K):
                row_copy(d_ref, buf, s, t, k).wait()
            return c
        lax.fori_loop(0, tt, body, 0)

    @pl.when(i == 0)
    def _():
        def body(t, c):
            for k in range(TOP_K):
                row_copy(dcur_ref, buf0_ref, 0, t, k).start()
            return c
        lax.fori_loop(0, tt, body, 0)

    def step(cur, nxt, s):
        drain(dcur_ref, cur, s)
        acc_ref[...] = ALPHA * x_ref[...] + jnp.dot(hs_ref[...], wd_ref[...], preferred_element_type=F32)

        def body(j, c):
            r0 = pl.multiple_of(j * tb, tb)
            for off in range(tb):
                for k in range(TOP_K):
                    row_copy(dnext_ref, nxt, 1 - s, r0 + off, k).start()
            rows = pl.ds(r0, tb)
            a = acc_ref[rows, :]
            hi_acc, lo_acc = a[:, :hcols], a[:, hcols:]
            for k in range(TOP_K):
                gate = w_ref[rows, k:k + 1]
                hi, lo = _unpack_halves(cur[k, rows, :])
                hi_acc = hi_acc + gate * hi
                lo_acc = lo_acc + gate * lo
            o = _layernorm(jnp.concatenate([hi_acc, lo_acc], axis=1), g_ref[...], b_ref[...])
            o_ref[rows, :] = o
            ob_ref[rows, :] = o.astype(BF16)
            return c
        lax.fori_loop(0, tt // tb, body, 0)

        @pl.when(i == nsteps - 1)
        def _():
            drain(dnext_ref, nxt, 1 - s)

    @pl.when(i % 2 == 0)
    def _():
        step(buf0_ref, buf1_ref, 0)

    @pl.when(i % 2 == 1)
    def _():
        step(buf1_ref, buf0_ref, 1)


def _combine(dest, gates, x1, hs, wd, g, b, yb, tt):
    t, d = x1.shape
    de = hs.shape[1]
    nsteps = t // tt
    smem = functools.partial(pl.BlockSpec, memory_space=pltpu.SMEM)
    row = pl.BlockSpec((tt, d), lambda i: (i, 0))
    vec = pl.BlockSpec((1, d), lambda i: (0, 0))
    return pl.pallas_call(
        functools.partial(_combine_body, nsteps=nsteps, tb=16), grid=(nsteps,),
        in_specs=[smem((TOP_K, tt), lambda i: (0, i)),
                  smem((TOP_K, tt), lambda i: (0, jnp.minimum(i + 1, nsteps - 1))),
                  pl.BlockSpec((tt, TOP_K), lambda i: (i, 0)), row,
                  pl.BlockSpec((tt, de), lambda i: (i, 0)), pl.BlockSpec((de, d), lambda i: (0, 0)),
                  vec, vec, pl.BlockSpec(memory_space=pl.ANY)],
        out_specs=[row, row],
        out_shape=[jax.ShapeDtypeStruct((t, d), F32), jax.ShapeDtypeStruct((t, d), BF16)],
        scratch_shapes=[pltpu.VMEM((TOP_K, tt, d // 2), U32), pltpu.VMEM((TOP_K, tt, d // 2), U32),
                        pltpu.VMEM((tt, d), F32), pltpu.SemaphoreType.DMA((2,))],
        compiler_params=_cparams(("arbitrary",), 48), name="moe_combine_ln2")(
            dest, dest, gates, x1, hs, wd, g, b, yb)


def _moe_ln2(x1, x1b, xw, router_wt, router_bias, wg, wu, wd, layer, shg, shu, shd, g2, b2):
    t, d = x1.shape
    ne = router_wt.shape[0]
    rows = MOE_ROWS
    nb = -(-(t * TOP_K + ne * (rows - 1)) // rows)
    nbp = -(-nb // LANES) * LANES
    eidx, gates, rank, cnt, crow = _route(x1, router_wt, router_bias, tm=512)
    pstart, pend, bexp, nused = _meta(cnt, crow, rows, nbp)
    dest = _dest(eidx, rank, pstart, tt=2048)
    xb = _dispatch(dest, cnt.astype(I32), pend, xw, nb * rows, rows, tt=256)
    hid = _exp_up(bexp[0], nused[0, :1], xb, wg, wu, layer, rows)
    yb = _exp_down(bexp[0], nused[0, :1], hid, wd, layer, rows)
    hs = _shared_up(x1b, shg, shu, tm=1024)
    return _combine(dest, gates.T, x1, hs, shd, g2, b2, yb, tt=128)


def kernel(x_prompt, x_sample, w_in, fnet_w, s5_a_re, s5_a_im, s5_log_dt, s5_b_re, s5_b_im, s5_c_re, s5_c_im,
           s5_d, glu_w, glu_b, gn_fnet, gn_s5, w_out, ln1_g, ln1_b, router_w, router_bias, exp_w_gate, exp_w_up,
           exp_w_down, sh_w_gate, sh_w_up, sh_w_down, ln2_g, ln2_b):
    bp, sp, d = x_prompt.shape
    bs, ss, _ = x_sample.shape
    depth = w_in.shape[0]
    heads, hd = fnet_w.shape[1], fnet_w.shape[2]
    fw = heads * hd
    n2 = 128
    seqs = [(0, bp, sp // n2, n2), (bp * sp, bs, ss // n2, n2)]
    seq_lens = [sp] * bp + [ss] * bs
    consts = {"chan": _chan_const(hd)}
    for (_, _, n1, n2_) in seqs:
        consts[(n1, n2_)] = _dft_consts(n1, n2_, hd)

    x = jnp.concatenate([x_prompt.reshape(bp * sp, d), x_sample.reshape(bs * ss, d)], axis=0)
    xb = x.astype(BF16)
    for l in range(depth):
        ops = _s5_prep(s5_a_re[l], s5_a_im[l], s5_log_dt[l], s5_b_re[l], s5_b_im[l], s5_c_re[l], s5_c_im[l], s5_d[l])
        h = _mm(xb, w_in[l].astype(BF16), tm=1024, tn=512, out_dtype=BF16, name="w_in")
        f = _fnet_branch(h, fnet_w[l].astype(BF16), seqs, consts, heads, hd)
        ys = _s5_branch(h, fw, ops, seq_lens)
        sm = _glu(ys, glu_w[l].astype(BF16), glu_b[l].reshape(1, -1), tm=1024, tn=1024)
        mix = _mixnorm(f, sm, gn_fnet[l].reshape(1, -1), gn_s5[l].reshape(1, -1), tm=512)
        y1 = _mm_res(mix, w_out[l].astype(BF16), x, tm=1024, tn=512, name="w_out")
        x1, x1b, xw = _ln1(y1, ln1_g[l].reshape(1, -1), ln1_b[l].reshape(1, -1), tm=256)
        x, xb = _moe_ln2(x1, x1b, xw, router_w[l].T, router_bias[l].reshape(-1, 1), exp_w_gate, exp_w_up,
                         exp_w_down, l, sh_w_gate[l].astype(BF16), sh_w_up[l].astype(BF16),
                         sh_w_down[l].astype(BF16), ln2_g[l].reshape(1, -1), ln2_b[l].reshape(1, -1))
    return (x[:bp * sp].reshape(bp, sp, d), x[bp * sp:].reshape(bs, ss, d))
```

```python
import functools
import math

import numpy as np
import jax
import jax.numpy as jnp
from jax import lax
from jax.experimental import pallas as pl
from jax.experimental.pallas import tpu as pltpu

F32 = jnp.float32
BF16 = jnp.bfloat16
I32 = jnp.int32
U32 = jnp.uint32

DEPTH = 4
FNET_HEADS = 8
S5_P = 16
S5_N = 64
N_EXPERTS = 64
N_EXP_GROUPS = 8
EXP_PER_GROUP = N_EXPERTS // N_EXP_GROUPS
TOPK_GROUPS = 4
TOP_K = 8
ROUTE_SCALE = 2.5
ALPHA = (2 * DEPTH) ** 0.25
EPS = 1e-5

S5_L = 16
S5_PAIR = 2
MOE_ROWS = 256
V7X_VMEM_BYTES = 64 * 1024 * 1024
VMEM_CAP = V7X_VMEM_BYTES - 8 * 1024 * 1024
LANES = 128


def _cparams(sem, vmem_mb):
    return pltpu.CompilerParams(dimension_semantics=sem,
                                vmem_limit_bytes=min(int(vmem_mb * 1024 * 1024), VMEM_CAP))


def _mm_body(x_ref, w_ref, o_ref):
    o_ref[...] = jnp.dot(x_ref[...], w_ref[...], preferred_element_type=F32).astype(o_ref.dtype)


def _mm(x, w, tm, tn, out_dtype, name):
    m, k = x.shape
    n = w.shape[1]
    return pl.pallas_call(
        _mm_body, grid=(m // tm, n // tn),
        in_specs=[pl.BlockSpec((tm, k), lambda i, j: (i, 0)), pl.BlockSpec((k, tn), lambda i, j: (0, j))],
        out_specs=pl.BlockSpec((tm, tn), lambda i, j: (i, j)),
        out_shape=jax.ShapeDtypeStruct((m, n), out_dtype),
        compiler_params=_cparams(("parallel", "parallel"), 48), name=name)(x, w)


def _glu_body(y_ref, w_ref, b_ref, yt_ref, o_ref):
    g = jnp.dot(y_ref[...], w_ref[...], preferred_element_type=F32) + b_ref[...]
    o_ref[...] = (yt_ref[...].astype(F32) * jax.nn.sigmoid(g)).astype(o_ref.dtype)


def _glu(y, w, b, tm, tn):
    m, k = y.shape
    n = w.shape[1]
    return pl.pallas_call(
        _glu_body, grid=(m // tm, n // tn),
        in_specs=[pl.BlockSpec((tm, k), lambda i, j: (i, 0)), pl.BlockSpec((k, tn), lambda i, j: (0, j)),
                  pl.BlockSpec((1, tn), lambda i, j: (0, j)), pl.BlockSpec((tm, tn), lambda i, j: (i, j))],
        out_specs=pl.BlockSpec((tm, tn), lambda i, j: (i, j)),
        out_shape=jax.ShapeDtypeStruct((m, n), BF16),
        compiler_params=_cparams(("parallel", "parallel"), 32), name="s5_glu")(y, w, b, y)


def _dft_consts(n1, n2, hd):
    n = n1 * n2
    k = np.arange(n1)
    ang1 = 2 * np.pi * np.outer(k, k) / n1
    c1, s1 = np.cos(ang1), np.sin(ang1)
    w1 = np.block([[c1, s1], [-s1, c1]])
    ang_t = 2 * np.pi * np.outer(np.arange(n2), np.arange(n1)) / n
    twr = np.broadcast_to(np.cos(ang_t)[:, :, None], (n2, n1, LANES))
    twi = np.broadcast_to(-np.sin(ang_t)[:, :, None], (n2, n1, LANES))
    k2 = np.arange(n2)
    ang2 = 2 * np.pi * np.outer(k2, k2) / n2
    w2 = np.concatenate([np.cos(ang2), np.sin(ang2)], axis=1) / math.sqrt(n)
    return (jnp.asarray(w1, BF16), jnp.asarray(np.ascontiguousarray(twr), F32),
            jnp.asarray(np.ascontiguousarray(twi), F32), jnp.asarray(w2, BF16))


def _chan_const(hd):
    c = np.arange(hd)
    ang = 2 * np.pi * np.outer(c, c) / hd
    return jnp.asarray(np.concatenate([np.cos(ang), -np.sin(ang)], axis=1) / math.sqrt(hd), BF16)


def _chan_body(x_ref, w_ref, zr_ref, zi_ref, *, hd):
    r = jnp.dot(x_ref[...], w_ref[...], preferred_element_type=F32)
    zr_ref[...] = r[:, :hd].astype(BF16)
    zi_ref[...] = r[:, hd:].astype(BF16)


def _fnet_chan(h, wc, heads, hd, tm):
    t = h.shape[0]
    shp = jax.ShapeDtypeStruct((t, heads * hd), BF16)
    return pl.pallas_call(
        functools.partial(_chan_body, hd=hd), grid=(t // tm, heads),
        in_specs=[pl.BlockSpec((tm, hd), lambda i, j: (i, j)), pl.BlockSpec((hd, 2 * hd), lambda i, j: (0, 0))],
        out_specs=[pl.BlockSpec((tm, hd), lambda i, j: (i, j))] * 2,
        out_shape=[shp, shp],
        compiler_params=_cparams(("parallel", "parallel"), 32), name="fnet_chan")(h, wc)


def _stage1_body(zr_ref, zi_ref, twr_ref, twi_ref, w_ref, yr_ref, yi_ref, *, n2t, n1, reps):
    w = w_ref[...]
    for j in range(n2t):
        z = jnp.concatenate([zr_ref[j], zi_ref[j]], axis=0)
        y = jnp.dot(w, z, preferred_element_type=F32)
        yr, yi = y[:n1], y[n1:]
        twr = jnp.concatenate([twr_ref[j]] * reps, axis=1)
        twi = jnp.concatenate([twi_ref[j]] * reps, axis=1)
        yr_ref[j] = (yr * twr - yi * twi).astype(BF16)
        yi_ref[j] = (yr * twi + yi * twr).astype(BF16)


def _fnet_stage1(zr, zi, w1, twr, twi, n2t, ct):
    nseq, n2, n1, c = zr.shape
    blk = pl.BlockSpec((None, n2t, n1, ct), lambda s, a, b: (s, a, 0, b))
    twb = pl.BlockSpec((n2t, n1, LANES), lambda s, a, b: (a, 0, 0))
    shp = jax.ShapeDtypeStruct(zr.shape, BF16)
    return pl.pallas_call(
        functools.partial(_stage1_body, n2t=n2t, n1=n1, reps=ct // LANES),
        grid=(nseq, n2 // n2t, c // ct),
        in_specs=[blk, blk, twb, twb, pl.BlockSpec((2 * n1, 2 * n1), lambda s, a, b: (0, 0))],
        out_specs=[blk, blk], out_shape=[shp, shp],
        compiler_params=_cparams(("parallel", "parallel", "parallel"), 32),
        name=f"fnet_stage1_n{n1}")(zr, zi, twr, twi, w1)


def _stage2_body(yr_ref, yi_ref, w_ref, wm_ref, f_ref, *, k1t):
    w = w_ref[...]
    wm = wm_ref[0]
    for j in range(k1t):
        yy = jnp.concatenate([yr_ref[j], yi_ref[j]], axis=0)
        r = jnp.dot(w, yy, preferred_element_type=F32)
        f_ref[j] = jnp.dot(r.astype(BF16), wm, preferred_element_type=F32).astype(BF16)


def _fnet_stage2(yr, yi, w2, wmap, k1t, hd):
    nseq, n1, n2, c = yr.shape
    blk = pl.BlockSpec((None, k1t, n2, hd), lambda s, a, h: (s, a, 0, h))
    return pl.pallas_call(
        functools.partial(_stage2_body, k1t=k1t), grid=(nseq, n1 // k1t, c // hd),
        in_specs=[blk, blk, pl.BlockSpec((n2, 2 * n2), lambda s, a, h: (0, 0)),
                  pl.BlockSpec((1, hd, hd), lambda s, a, h: (h, 0, 0))],
        out_specs=blk, out_shape=jax.ShapeDtypeStruct(yr.shape, BF16),
        compiler_params=_cparams(("parallel", "parallel", "parallel"), 32),
        name=f"fnet_stage2_n{n1}")(yr, yi, w2, wmap)


def _fnet_branch(h, wmap_b, seqs, consts, heads, hd):
    c = heads * hd
    zr, zi = _fnet_chan(h, consts["chan"], heads, hd, tm=2048)
    outs = []
    for (row0, nseq, n1, n2) in seqs:
        rows = nseq * n1 * n2
        w1, twr, twi, w2 = consts[(n1, n2)]

        def to_stage1(z):
            return z[row0:row0 + rows].reshape(nseq, n1, n2, c).transpose(0, 2, 1, 3)

        yr, yi = _fnet_stage1(to_stage1(zr), to_stage1(zi), w1, twr, twi, n2t=8, ct=min(c, 1024))
        yr = yr.transpose(0, 2, 1, 3)
        yi = yi.transpose(0, 2, 1, 3)
        f = _fnet_stage2(yr, yi, w2, wmap_b, k1t=8, hd=hd)
        outs.append(f.transpose(0, 2, 1, 3).reshape(rows, c))
    return jnp.concatenate(outs, axis=0)


def _s5_prep_body(are_ref, aim_ref, ldt_ref, brt_ref, bit_ref, cr_ref, ci_ref, crt_ref, cit_ref, dd_ref,
                  m_ref, p_ref, q_ref, lam_ref):
    L, P = S5_L, S5_P
    R = L * P
    lane = lax.broadcasted_iota(I32, (R, LANES), 1)
    member = lane // S5_N
    slot = lane // P
    row = lax.broadcasted_iota(I32, (L, LANES), 0).astype(F32)

    def powers(ar, ai, dt, tau):
        e = jnp.exp(tau * (ar * dt))
        ang = tau * (ai * dt)
        return e * jnp.cos(ang), e * jnp.sin(ang)

    def expand(x):
        return jnp.concatenate([jnp.broadcast_to(x[t:t + 1], (P, LANES)) for t in range(L)], axis=0)

    def tile(x):
        return jnp.concatenate([x] * L, axis=0)

    def cmul(ar_, ai_, br_, bi_):
        return ar_ * br_ - ai_ * bi_, ar_ * bi_ + ai_ * br_

    gs, qs, kts, lams = [], [], [], []
    for d in (0, 1):
        ar, ai = are_ref[d], aim_ref[d]
        dt = jnp.exp(ldt_ref[d])
        one = jnp.ones((1, LANES), F32)
        lr, li = powers(ar, ai, dt, one)
        den = ar * ar + ai * ai
        xr, xi = lr - 1.0, li
        cfr = (xr * ar + xi * ai) / den
        cfi = (xi * ar - xr * ai) / den
        bbr, bbi = cmul(cfr, cfi, brt_ref[d], bit_ref[d])
        tau_g = (L - 1.0 - row) if d == 0 else row
        tau_q = (row + 1.0) if d == 0 else (L - row)
        pr, pi = powers(ar, ai, dt, tau_g)
        gr, gi = cmul(expand(pr), expand(pi), tile(bbr), tile(bbi))
        qr, qi = powers(ar, ai, dt, tau_q)
        er, ei = cmul(tile(cr_ref[d]), tile(ci_ref[d]), expand(qr), expand(qi))
        gs.append((gr, gi))
        qs.append((er, -ei))
        lams.extend(powers(ar, ai, dt, float(L) * one))
        rhs = jnp.concatenate([crt_ref[d], cit_ref[d]], axis=0)
        per_member = []
        for g in range(S5_PAIR):
            msk = member == g
            lhs = jnp.concatenate([jnp.where(msk, gr, 0.0), jnp.where(msk, -gi, 0.0)], axis=1)
            per_member.append(jnp.dot(lhs, rhs, preferred_element_type=F32, precision=lax.Precision.HIGHEST))
        kts.append(per_member)

    m_ref[...] = jnp.zeros(m_ref.shape, m_ref.dtype)
    for g in range(S5_PAIR):
        ktf, ktb = kts[0][g], kts[1][g]
        v = jnp.concatenate([ktf[:R - P], ktf[R - P:] + ktb[:P] + dd_ref[g], ktb[P:]], axis=0)
        for hh in range(R // LANES):
            acc = jnp.zeros((R, LANES), F32)
            for il in range(LANES // P):
                i = hh * (LANES // P) + il
                acc = jnp.where(slot == il, v[(L - 1 - i) * P:(L - 1 - i) * P + R], acc)
            m_ref[g * R:(g + 1) * R, g * R + hh * LANES:g * R + (hh + 1) * LANES] = acc.astype(m_ref.dtype)
        msk = member == g
        comps_p = [gs[0][0], gs[0][1], gs[1][0], gs[1][1]]
        comps_q = [qs[0][0], qs[0][1], qs[1][0], qs[1][1]]
        for c in range(4):
            p_ref[g * R:(g + 1) * R, c * LANES:(c + 1) * LANES] = jnp.where(msk, comps_p[c], 0.0).astype(p_ref.dtype)
            q_ref[g * R:(g + 1) * R, c * LANES:(c + 1) * LANES] = jnp.where(msk, comps_q[c], 0.0).astype(q_ref.dtype)
    lam_ref[...] = jnp.concatenate(lams, axis=1)


def _s5_prep(a_re, a_im, log_dt, b_re, b_im, c_re, c_im, d_skip):
    g, n, p = b_re.shape[1], b_re.shape[2], b_re.shape[3]
    sg = g // S5_PAIR
    assert n == S5_N and p == S5_P and S5_PAIR * n == LANES
    lanes4 = lambda x: x.reshape(2, sg, 1, LANES).astype(F32)
    are, aim = lanes4(a_re), lanes4(a_im)
    ldt = lanes4(jnp.repeat(log_dt, n, axis=1))
    bt = lambda x: x.reshape(2, sg, S5_PAIR, n, p).transpose(0, 1, 4, 2, 3).reshape(2, sg, p, LANES).astype(F32)
    cc = lambda x: x.reshape(2, sg, S5_PAIR, p, n).transpose(0, 1, 3, 2, 4).reshape(2, sg, p, LANES).astype(F32)
    ct = lambda x: jnp.tile(x.reshape(2, sg, S5_PAIR, p, n).transpose(0, 1, 2, 4, 3).reshape(2, sg, LANES, p),
                            (1, 1, 1, LANES // p)).astype(F32)
    eye = jnp.eye(p, dtype=F32)
    dd = d_skip.reshape(sg, S5_PAIR, 1, p).astype(F32) * eye[None, None]
    dd = jnp.tile(dd, (1, 1, 1, LANES // p))
    r2 = S5_PAIR * S5_L * p
    v1 = pl.BlockSpec((2, None, 1, LANES), lambda s: (0, s, 0, 0))
    vp = pl.BlockSpec((2, None, p, LANES), lambda s: (0, s, 0, 0))
    vt = pl.BlockSpec((2, None, LANES, LANES), lambda s: (0, s, 0, 0))
    mat = pl.BlockSpec((None, r2, r2), lambda s: (s, 0, 0))
    mshape = jax.ShapeDtypeStruct((sg, r2, r2), BF16)
    return pl.pallas_call(
        _s5_prep_body, grid=(sg,),
        in_specs=[v1, v1, v1, vp, vp, vp, vp, vt, vt, pl.BlockSpec((None, S5_PAIR, p, LANES), lambda s: (s, 0, 0, 0))],
        out_specs=[mat, mat, mat, pl.BlockSpec((None, 1, 4 * LANES), lambda s: (s, 0, 0))],
        out_shape=[mshape, mshape, mshape, jax.ShapeDtypeStruct((sg, 1, 4 * LANES), F32)],
        compiler_params=_cparams(("parallel",), 32), name="s5_prep")(
            are, aim, ldt, bt(b_re), bt(b_im), cc(c_re), cc(c_im), ct(c_re), ct(c_im), dd)


def _s5_statein(u, pm, nct):
    sg, nc, r2 = u.shape
    return pl.pallas_call(
        _mm_body, grid=(sg, nc // nct),
        in_specs=[pl.BlockSpec((None, nct, r2), lambda s, c: (s, c, 0)),
                  pl.BlockSpec((None, r2, pm.shape[2]), lambda s, c: (s, 0, 0))],
        out_specs=pl.BlockSpec((None, nct, pm.shape[2]), lambda s, c: (s, c, 0)),
        out_shape=jax.ShapeDtypeStruct((sg, nc, pm.shape[2]), F32),
        compiler_params=_cparams(("parallel", "parallel"), 32), name="s5_statein")(u, pm)


def _s5_scan_body(fk_ref, bk_ref, rs_ref, pf_ref, pb_ref, lam_ref, sf_ref, sb_ref, st_ref, *, nct):
    k = pl.program_id(0)

    @pl.when(rs_ref[k] == 1)
    def _():
        st_ref[...] = jnp.zeros(st_ref.shape, F32)

    lam = lam_ref[...]
    lfr, lfi = lam[:, 0:LANES], lam[:, LANES:2 * LANES]
    lbr, lbi = lam[:, 2 * LANES:3 * LANES], lam[:, 3 * LANES:]

    def step(i, carry):
        fr, fi, br, bi = carry
        cb = nct - 1 - i
        sf_ref[i] = jnp.concatenate([fr, fi], axis=1).astype(BF16)
        sb_ref[cb] = jnp.concatenate([br, bi], axis=1).astype(BF16)
        pf = pf_ref[i]
        pb = pb_ref[cb]
        fr2 = lfr * fr - lfi * fi + pf[:, :LANES]
        fi2 = lfr * fi + lfi * fr + pf[:, LANES:]
        br2 = lbr * br - lbi * bi + pb[:, :LANES]
        bi2 = lbr * bi + lbi * br + pb[:, LANES:]
        return fr2, fi2, br2, bi2

    out = lax.fori_loop(0, nct, step, (st_ref[0], st_ref[1], st_ref[2], st_ref[3]))
    for j in range(4):
        st_ref[j] = out[j]


def _s5_scan(pin, lam, seq_chunks, nct):
    nc, sg, _ = pin.shape
    fk, bk, rs = [], [], []
    base = 0
    for n_chunks in seq_chunks:
        nb = n_chunks // nct
        for j in range(nb):
            fk.append(base + j)
            bk.append(base + nb - 1 - j)
            rs.append(1 if j == 0 else 0)
        base += nb
    nsteps = len(fk)
    to = lambda v: jnp.asarray(np.asarray(v, np.int32))
    blk_f = pl.BlockSpec((nct, sg, 2 * LANES), lambda k, fk_, bk_, rs_: (fk_[k], 0, 0))
    blk_b = pl.BlockSpec((nct, sg, 2 * LANES), lambda k, fk_, bk_, rs_: (bk_[k], 0, 1))
    out_f = pl.BlockSpec((nct, sg, 2 * LANES), lambda k, fk_, bk_, rs_: (fk_[k], 0, 0))
    out_b = pl.BlockSpec((nct, sg, 2 * LANES), lambda k, fk_, bk_, rs_: (bk_[k], 0, 0))
    shp = jax.ShapeDtypeStruct((nc, sg, 2 * LANES), BF16)
    return pl.pallas_call(
        functools.partial(_s5_scan_body, nct=nct),
        grid_spec=pltpu.PrefetchScalarGridSpec(
            num_scalar_prefetch=3, grid=(nsteps,),
            in_specs=[blk_f, blk_b, pl.BlockSpec((sg, 4 * LANES), lambda k, *_: (0, 0))],
            out_specs=[out_f, out_b],
            scratch_shapes=[pltpu.VMEM((4, sg, LANES), F32)]),
        out_shape=[shp, shp],
        compiler_params=_cparams(("arbitrary",), 40), name="s5_scan")(to(fk), to(bk), to(rs), pin, pin, lam)


def _s5_out_body(u_ref, m_ref, q_ref, sf_ref, sb_ref, y_ref):
    y = jnp.dot(u_ref[...], m_ref[...], preferred_element_type=F32)
    s = jnp.concatenate([sf_ref[...], sb_ref[...]], axis=1)
    y = y + lax.dot_general(s, q_ref[...], (((1,), (1,)), ((), ())), preferred_element_type=F32)
    y_ref[...] = jax.nn.gelu(y).astype(y_ref.dtype)


def _s5_out(u, mm, qt, sf, sb, nct):
    sg, nc, r2 = u.shape
    ublk = pl.BlockSpec((None, nct, r2), lambda s, c: (s, c, 0))
    sblk = pl.BlockSpec((None, nct, 2 * LANES), lambda s, c: (s, c, 0))
    mblk = pl.BlockSpec((None, r2, r2), lambda s, c: (s, 0, 0))
    return pl.pallas_call(
        _s5_out_body, grid=(sg, nc // nct),
        in_specs=[ublk, mblk, mblk, sblk, sblk], out_specs=ublk,
        out_shape=jax.ShapeDtypeStruct(u.shape, BF16),
        compiler_params=_cparams(("parallel", "parallel"), 32), name="s5_out")(u, mm, qt, sf, sb)


def _s5_perm_const():
    half = S5_L // 2
    ppl = LANES // (S5_PAIR * S5_P)
    n = half * LANES
    e = np.zeros((n, n), np.float32)
    for sl in range(half):
        for pr in range(ppl):
            for g in range(S5_PAIR):
                for q in range(S5_P):
                    src = sl * LANES + pr * S5_PAIR * S5_P + g * S5_P + q
                    dst = (pr * S5_PAIR + g) * LANES + sl * S5_P + q
                    e[src, dst] = 1.0
    return e


def _s5_flatten_body(x_ref, e_ref, u_ref, scr_ref, *, nct):
    half = S5_L // 2
    r = S5_L * S5_P
    scr_ref[...] = x_ref[...].astype(F32).reshape(nct * S5_L, LANES)
    e = e_ref[...]
    for hh in range(2):
        pieces = [scr_ref[pl.ds(hh * half + sl, nct, stride=S5_L), :].astype(BF16) for sl in range(half)]
        res = jnp.dot(jnp.concatenate(pieces, axis=1), e, preferred_element_type=F32).astype(BF16)
        for j in range(LANES // S5_P):
            pr, g = divmod(j, S5_PAIR)
            u_ref[pr, :, g * r + hh * LANES:g * r + (hh + 1) * LANES] = res[:, j * LANES:(j + 1) * LANES]


def _s5_flatten(h, col0, w, nct):
    t = h.shape[0]
    nc = t // S5_L
    ppl = LANES // (S5_PAIR * S5_P)
    sg = w // (S5_PAIR * S5_P)
    r2 = S5_PAIR * S5_L * S5_P
    e = jnp.asarray(_s5_perm_const(), BF16)
    return pl.pallas_call(
        functools.partial(_s5_flatten_body, nct=nct), grid=(nc // nct, w // LANES),
        in_specs=[pl.BlockSpec((nct, S5_L, LANES), lambda c, l: (c, 0, col0 // LANES + l)),
                  pl.BlockSpec(e.shape, lambda c, l: (0, 0))],
        out_specs=pl.BlockSpec((ppl, nct, r2), lambda c, l: (l, c, 0)),
        out_shape=jax.ShapeDtypeStruct((sg, nc, r2), BF16),
        scratch_shapes=[pltpu.VMEM((nct * S5_L, LANES), F32)],
        compiler_params=_cparams(("parallel", "parallel"), 32), name="s5_flatten")(
            h.reshape(nc, S5_L, h.shape[1]), e)


def _s5_unflatten_body(y_ref, et_ref, o_ref, scr_ref, *, nct):
    half = S5_L // 2
    r = S5_L * S5_P
    et = et_ref[...]
    for hh in range(2):
        parts = []
        for j in range(LANES // S5_P):
            pr, g = divmod(j, S5_PAIR)
            parts.append(y_ref[pr, :, g * r + hh * LANES:g * r + (hh + 1) * LANES])
        res = jnp.dot(jnp.concatenate(parts, axis=1), et, preferred_element_type=F32)
        for sl in range(half):
            scr_ref[pl.ds(hh * half + sl, nct, stride=S5_L), :] = res[:, sl * LANES:(sl + 1) * LANES]
    o_ref[...] = scr_ref[...].reshape(nct, S5_L, LANES).astype(o_ref.dtype)


def _s5_unflatten(yg, nct):
    sg, nc, r2 = yg.shape
    ppl = LANES // (S5_PAIR * S5_P)
    w = sg * S5_PAIR * S5_P
    et = jnp.asarray(_s5_perm_const().T, BF16)
    out = pl.pallas_call(
        functools.partial(_s5_unflatten_body, nct=nct), grid=(nc // nct, w // LANES),
        in_specs=[pl.BlockSpec((ppl, nct, r2), lambda c, l: (l, c, 0)), pl.BlockSpec(et.shape, lambda c, l: (0, 0))],
        out_specs=pl.BlockSpec((nct, S5_L, LANES), lambda c, l: (c, 0, l)),
        out_shape=jax.ShapeDtypeStruct((nc, S5_L, w), BF16),
        scratch_shapes=[pltpu.VMEM((nct * S5_L, LANES), F32)],
        compiler_params=_cparams(("parallel", "parallel"), 32), name="s5_unflatten")(yg, et)
    return out.reshape(nc * S5_L, w)


def _s5_branch(h, col0, ops, seq_lens):
    mm, pm, qt, lam = ops
    t = h.shape[0]
    sg = mm.shape[0]
    w = sg * S5_PAIR * S5_P
    nc = t // S5_L
    u = _s5_flatten(h, col0, w, nct=256)
    pin = _s5_statein(u, pm, nct=nc // 2).transpose(1, 0, 2)
    sf, sb = _s5_scan(pin, lam.reshape(sg, 4 * LANES), [s // S5_L for s in seq_lens], nct=64)
    yg = _s5_out(u, mm, qt, sf.transpose(1, 0, 2), sb.transpose(1, 0, 2), nct=nc // 2)
    return _s5_unflatten(yg, nct=256)


def _mix_out_body(f_ref, s_ref, gf_ref, gs_ref, w_ref, r_ref, o_ref, mix_ref, *, w):
    @pl.when(pl.program_id(1) == 0)
    def _():
        def rn(x, g):
            xf = x.astype(F32)
            return (xf * lax.rsqrt(jnp.mean(xf * xf, axis=-1, keepdims=True) + EPS) * g).astype(mix_ref.dtype)
        mix_ref[:, :w] = rn(f_ref[...], gf_ref[...])
        mix_ref[:, w:] = rn(s_ref[...], gs_ref[...])

    o_ref[...] = ALPHA * r_ref[...] + jnp.dot(mix_ref[...], w_ref[...], preferred_element_type=F32)


def _mix_out(f, s, gf, gs, wo, res, tm, tn):
    t, w = f.shape
    ws = s.shape[1]
    n = wo.shape[1]
    return pl.pallas_call(
        functools.partial(_mix_out_body, w=w), grid=(t // tm, n // tn),
        in_specs=[pl.BlockSpec((tm, w), lambda i, j: (i, 0)), pl.BlockSpec((tm, ws), lambda i, j: (i, 0)),
                  pl.BlockSpec((1, w), lambda i, j: (0, 0)), pl.BlockSpec((1, ws), lambda i, j: (0, 0)),
                  pl.BlockSpec((w + ws, tn), lambda i, j: (0, j)), pl.BlockSpec((tm, tn), lambda i, j: (i, j))],
        out_specs=pl.BlockSpec((tm, tn), lambda i, j: (i, j)),
        out_shape=jax.ShapeDtypeStruct((t, n), F32),
        scratch_shapes=[pltpu.VMEM((tm, w + ws), BF16)],
        compiler_params=_cparams(("parallel", "arbitrary"), 48), name="w_out")(f, s, gf, gs, wo, res)


def _layernorm(y, g, b):
    mu = jnp.mean(y, axis=-1, keepdims=True)
    yc = y - mu
    var = jnp.mean(yc * yc, axis=-1, keepdims=True)
    return yc * lax.rsqrt(var + EPS) * g + b


def _pack_halves(ob):
    hcols = ob.shape[1] // 2
    hi = lax.bitcast_convert_type(ob[:, :hcols].astype(F32), U32)
    lo = lax.bitcast_convert_type(ob[:, hcols:].astype(F32), U32)
    return hi | (lo >> 16)


def _unpack_halves(words):
    hi = lax.bitcast_convert_type(words & jnp.uint32(0xFFFF0000), F32)
    lo = lax.bitcast_convert_type(words << 16, F32)
    return hi, lo


def _ln1_body(y_ref, g_ref, b_ref, x_ref, xw_ref):
    o = _layernorm(y_ref[...], g_ref[...], b_ref[...])
    x_ref[...] = o
    xw_ref[...] = _pack_halves(o.astype(BF16))


def _ln1(y, g, b, tm):
    t, d = y.shape
    row = pl.BlockSpec((tm, d), lambda i: (i, 0))
    vec = pl.BlockSpec((1, d), lambda i: (0, 0))
    return pl.pallas_call(
        _ln1_body, grid=(t // tm,), in_specs=[row, vec, vec],
        out_specs=[row, pl.BlockSpec((tm, d // 2), lambda i: (i, 0))],
        out_shape=[jax.ShapeDtypeStruct((t, d), F32), jax.ShapeDtypeStruct((t, d // 2), U32)],
        compiler_params=_cparams(("parallel",), 48), name="ln1")(y, g, b)


def _route_body(x_ref, wt_ref, bias_ref, tri_ref, e_ref, w_ref, r_ref, cnt_ref, crow_ref, carry_ref, rowc_ref):
    i = pl.program_id(0)

    @pl.when(i == 0)
    def _():
        carry_ref[...] = jnp.zeros(carry_ref.shape, F32)
        rowc_ref[...] = jnp.zeros(rowc_ref.shape, F32)

    ne = wt_ref.shape[0]
    tm = x_ref.shape[0]
    x = x_ref[...]
    xh = x.astype(BF16)
    xl = (x - xh.astype(F32)).astype(BF16)
    wt = wt_ref[...]
    wh = wt.astype(BF16)
    wl = (wt - wh.astype(F32)).astype(BF16)
    dn = (((1,), (1,)), ((), ()))
    dg = lambda a, b: lax.dot_general(a, b, dn, preferred_element_type=F32)
    logits = dg(wh, xh) + (dg(wh, xl) + dg(wl, xh))
    scores = jax.nn.sigmoid(logits)
    sel = scores + bias_ref[...]
    neg = jnp.float32(-jnp.inf)

    sub = lax.broadcasted_iota(I32, (EXP_PER_GROUP, tm), 0)
    gscore = []
    for g in range(N_EXP_GROUPS):
        blk = sel[g * EXP_PER_GROUP:(g + 1) * EXP_PER_GROUP]
        m1 = jnp.max(blk, axis=0, keepdims=True)
        i1 = jnp.min(jnp.where(blk == m1, sub, EXP_PER_GROUP), axis=0, keepdims=True)
        m2 = jnp.max(jnp.where(sub == i1, neg, blk), axis=0, keepdims=True)
        gscore.append(m1 + m2)
    keep = []
    for g in range(N_EXP_GROUPS):
        beat = jnp.zeros((1, tm), I32)
        for g2 in range(N_EXP_GROUPS):
            if g2 == g:
                continue
            wins = (gscore[g2] > gscore[g]) | ((gscore[g2] == gscore[g]) & (g2 < g))
            beat = beat + wins.astype(I32)
        keep.append(beat < TOPK_GROUPS)
    cand = jnp.concatenate(
        [jnp.where(keep[g], sel[g * EXP_PER_GROUP:(g + 1) * EXP_PER_GROUP], neg) for g in range(N_EXP_GROUPS)],
        axis=0)
    eid = lax.broadcasted_iota(I32, (ne, tm), 0)
    onehot = jnp.zeros((ne, tm), F32)
    idxs, wts = [], []
    for _ in range(TOP_K):
        m = jnp.max(cand, axis=0, keepdims=True)
        idx = jnp.min(jnp.where(cand == m, eid, ne), axis=0, keepdims=True)
        hit = eid == idx
        idxs.append(idx)
        wts.append(jnp.sum(jnp.where(hit, scores, 0.0), axis=0, keepdims=True))
        onehot = onehot + hit.astype(F32)
        cand = jnp.where(hit, neg, cand)
    wsum = wts[0]
    for k in range(1, TOP_K):
        wsum = wsum + wts[k]
    oh_b = onehot.astype(BF16)
    pre = jnp.dot(oh_b, tri_ref[...], preferred_element_type=F32) + carry_ref[:, 0:1]
    for k in range(TOP_K):
        e_ref[k:k + 1, :] = idxs[k]
        w_ref[k:k + 1, :] = wts[k] / wsum * ROUTE_SCALE
        r_ref[k:k + 1, :] = jnp.sum(jnp.where(eid == idxs[k], pre, 0.0), axis=0, keepdims=True).astype(I32)
    carry_ref[...] = carry_ref[...] + jnp.sum(onehot, axis=1, keepdims=True)
    rowc_ref[...] = rowc_ref[...] + dg(jnp.ones((8, tm), BF16), oh_b)
    cnt_ref[...] = carry_ref[...]
    crow_ref[...] = rowc_ref[...]


def _route(x, wt, bias, tm):
    t, d = x.shape
    ne = wt.shape[0]
    tri = jnp.asarray(np.triu(np.ones((tm, tm), np.float32), k=1), BF16)
    kt = pl.BlockSpec((TOP_K, tm), lambda i: (0, i))
    return pl.pallas_call(
        _route_body, grid=(t // tm,),
        in_specs=[pl.BlockSpec((tm, d), lambda i: (i, 0)), pl.BlockSpec((ne, d), lambda i: (0, 0)),
                  pl.BlockSpec((ne, 1), lambda i: (0, 0)), pl.BlockSpec((tm, tm), lambda i: (0, 0))],
        out_specs=[kt, kt, kt, pl.BlockSpec((ne, LANES), lambda i: (0, 0)), pl.BlockSpec((8, ne), lambda i: (0, 0))],
        out_shape=[jax.ShapeDtypeStruct((TOP_K, t), I32), jax.ShapeDtypeStruct((TOP_K, t), F32),
                   jax.ShapeDtypeStruct((TOP_K, t), I32), jax.ShapeDtypeStruct((ne, LANES), F32),
                   jax.ShapeDtypeStruct((8, ne), F32)],
        scratch_shapes=[pltpu.VMEM((ne, LANES), F32), pltpu.VMEM((8, ne), F32)],
        compiler_params=_cparams(("arbitrary",), 40), name="moe_route")(x, wt, bias, tri)


def _meta_body(cnt_ref, crow_ref, ps_ref, pe_ref, be_ref, nu_ref, *, rows, nbp):
    ne = cnt_ref.shape[0]
    pad = lambda c: jnp.floor((c + (rows - 1)) / rows) * rows
    prow = pad(crow_ref[0:1, :])
    pcol = pad(cnt_ref[:, 0:1])
    er = lax.broadcasted_iota(I32, (ne, ne), 0)
    ec = lax.broadcasted_iota(I32, (ne, ne), 1)
    pstart = jnp.sum(jnp.where(ec < er, prow, 0.0), axis=1, keepdims=True)
    pend = pstart + pcol
    ps_ref[...] = jnp.broadcast_to(pstart, ps_ref.shape).astype(I32)
    pe_ref[...] = jnp.broadcast_to(pend, pe_ref.shape).astype(I32)
    bstart = lax.broadcasted_iota(I32, (ne, nbp), 1).astype(F32) * rows
    be = jnp.sum((pend <= bstart).astype(F32), axis=0, keepdims=True)
    be_ref[...] = jnp.broadcast_to(jnp.minimum(be, ne - 1.0), be_ref.shape).astype(I32)
    total = jnp.sum(prow, axis=1, keepdims=True) / rows
    nu_ref[...] = jnp.broadcast_to(total, nu_ref.shape).astype(I32)


def _meta(cnt, crow, rows, nbp):
    ne = cnt.shape[0]
    return pl.pallas_call(
        functools.partial(_meta_body, rows=rows, nbp=nbp),
        out_shape=[jax.ShapeDtypeStruct((ne, LANES), I32), jax.ShapeDtypeStruct((ne, LANES), I32),
                   jax.ShapeDtypeStruct((8, nbp), I32), jax.ShapeDtypeStruct((8, LANES), I32)],
        name="moe_meta")(cnt, crow)


def _dest_body(e_ref, r_ref, ps_ref, d_ref):
    ne = ps_ref.shape[0]
    tt = e_ref.shape[1]
    eid = lax.broadcasted_iota(I32, (ne, tt), 0)
    ps = ps_ref[:, 0:1].astype(F32)
    for k in range(TOP_K):
        start = jnp.sum(jnp.where(eid == e_ref[k:k + 1, :], ps, 0.0), axis=0, keepdims=True)
        d_ref[k:k + 1, :] = start.astype(I32) + r_ref[k:k + 1, :]


def _dest(e, r, ps, tt):
    t = e.shape[1]
    kt = pl.BlockSpec((TOP_K, tt), lambda i: (0, i))
    return pl.pallas_call(
        _dest_body, grid=(t // tt,),
        in_specs=[kt, kt, pl.BlockSpec(ps.shape, lambda i: (0, 0))], out_specs=kt,
        out_shape=jax.ShapeDtypeStruct((TOP_K, t), I32),
        compiler_params=_cparams(("parallel",), 16), name="moe_dest")(e, r, ps)


def _dispatch_body(dest_ref, cnt_ref, pe_ref, xw_ref, xb_ref, zero_ref, sem, zsem, *, rows):
    i = pl.program_id(0)
    tt = xw_ref.shape[0]
    ne = pe_ref.shape[0]

    def zcopy(e):
        start = pl.multiple_of(pe_ref[e, 0] - rows, rows)
        return pltpu.make_async_copy(zero_ref, xb_ref.at[pl.ds(start, rows), :], zsem)

    @pl.when(i == 0)
    def _():
        zero_ref[...] = jnp.zeros(zero_ref.shape, zero_ref.dtype)

        def start(e, c):
            @pl.when(cnt_ref[e, 0] > 0)
            def _():
                zcopy(e).start()
            return c

        def wait(e, c):
            @pl.when(cnt_ref[e, 0] > 0)
            def _():
                zcopy(e).wait()
            return c

        lax.fori_loop(0, ne, start, 0)
        lax.fori_loop(0, ne, wait, 0)

    def row_copy(t, k):
        return pltpu.make_async_copy(xw_ref.at[pl.ds(t, 1), :], xb_ref.at[pl.ds(dest_ref[k, t], 1), :], sem)

    def issue(t, c):
        for k in range(TOP_K):
            row_copy(t, k).start()
        return c

    def drain(t, c):
        for k in range(TOP_K):
            row_copy(t, k).wait()
        return c

    lax.fori_loop(0, tt, issue, 0)
    lax.fori_loop(0, tt, drain, 0)


def _dispatch(dest, cnt_i, pend, xw, n_slots, rows, tt):
    t, hw = xw.shape
    smem = functools.partial(pl.BlockSpec, memory_space=pltpu.SMEM)
    return pl.pallas_call(
        functools.partial(_dispatch_body, rows=rows), grid=(t // tt,),
        in_specs=[smem((TOP_K, tt), lambda i: (0, i)), smem(cnt_i.shape, lambda i: (0, 0)),
                  smem(pend.shape, lambda i: (0, 0)), pl.BlockSpec((tt, hw), lambda i: (i, 0))],
        out_specs=pl.BlockSpec(memory_space=pl.ANY),
        out_shape=jax.ShapeDtypeStruct((n_slots, hw), U32),
        scratch_shapes=[pltpu.VMEM((rows, hw), U32), pltpu.SemaphoreType.DMA(()), pltpu.SemaphoreType.DMA(())],
        compiler_params=_cparams(("arbitrary",), 24), name="moe_dispatch")(dest, cnt_i, pend, xw)


def _silu_mul(g, u):
    return jax.nn.silu(g) * u


def _expert_weights_step(i, be_ref, nu_ref, layer, hbm_refs, stage_refs, bf_refs, sem):
    nu = nu_ref[0]
    e = be_ref[i]
    fresh = (i == 0) | (e != be_ref[jnp.maximum(i - 1, 0)])

    def copies(ex):
        return [pltpu.make_async_copy(h.at[layer, ex], st, sem.at[n])
                for n, (h, st) in enumerate(zip(hbm_refs, stage_refs))]

    @pl.when((i < nu) & fresh)
    def _():
        @pl.when(i == 0)
        def _():
            for c in copies(e):
                c.start()

        for c in copies(e):
            c.wait()
        for st, bf in zip(stage_refs, bf_refs):
            bf[...] = st[...].astype(BF16)
        nxt = lax.while_loop(lambda j: (j < nu) & (be_ref[jnp.minimum(j, nu - 1)] == e), lambda j: j + 1, i + 1)

        @pl.when(nxt < nu)
        def _():
            for c in copies(be_ref[jnp.minimum(nxt, nu - 1)]):
                c.start()


def _experts_body(be_ref, nu_ref, xb_ref, wg_ref, wu_ref, wd_ref, y_ref, wgf_ref, wuf_ref, wdf_ref,
                  wgb_ref, wub_ref, wdb_ref, sem, *, layer):
    i = pl.program_id(0)
    valid = i < nu_ref[0]
    _expert_weights_step(i, be_ref, nu_ref, layer, (wg_ref, wu_ref, wd_ref), (wgf_ref, wuf_ref, wdf_ref),
                         (wgb_ref, wub_ref, wdb_ref), sem)

    @pl.when(valid)
    def _():
        hcols = xb_ref.shape[1]
        hi, lo = _unpack_halves(xb_ref[...])
        xa, xc = hi.astype(BF16), lo.astype(BF16)
        dot = lambda a, w: jnp.dot(a, w, preferred_element_type=F32)
        g = dot(xa, wgb_ref[:hcols]) + dot(xc, wgb_ref[hcols:])
        u = dot(xa, wub_ref[:hcols]) + dot(xc, wub_ref[hcols:])
        hid = _silu_mul(g, u).astype(BF16)
        y_ref[...] = _pack_halves(dot(hid, wdb_ref[...]).astype(BF16))

    @pl.when(jnp.logical_not(valid))
    def _():
        y_ref[...] = jnp.zeros(y_ref.shape, y_ref.dtype)


def _experts(bexp, nused, xb, wg, wu, wd, layer, rows):
    n_slots, hw = xb.shape
    _, ne, d, de = wg.shape
    nb = n_slots // rows
    last = lambda i, be, nu: jnp.minimum(i, nu[0] - 1)
    wspec = pl.BlockSpec(memory_space=pl.ANY)
    return pl.pallas_call(
        functools.partial(_experts_body, layer=layer),
        grid_spec=pltpu.PrefetchScalarGridSpec(
            num_scalar_prefetch=2, grid=(nb,),
            in_specs=[pl.BlockSpec((rows, hw), lambda i, be, nu: (last(i, be, nu), 0)), wspec, wspec, wspec],
            out_specs=pl.BlockSpec((rows, hw), lambda i, be, nu: (i, 0)),
            scratch_shapes=[pltpu.VMEM((d, de), F32), pltpu.VMEM((d, de), F32), pltpu.VMEM((de, d), F32),
                            pltpu.VMEM((d, de), BF16), pltpu.VMEM((d, de), BF16), pltpu.VMEM((de, d), BF16),
                            pltpu.SemaphoreType.DMA((3,))]),
        out_shape=jax.ShapeDtypeStruct((n_slots, hw), U32),
        compiler_params=_cparams(("arbitrary",), 56), name="moe_experts")(bexp, nused, xb, wg, wu, wd)


def _shared_up_body(xw_ref, wg_ref, wu_ref, o_ref):
    hcols = xw_ref.shape[1]
    hi, lo = _unpack_halves(xw_ref[...])
    xa, xc = hi.astype(BF16), lo.astype(BF16)
    dot = lambda a, w: jnp.dot(a, w, preferred_element_type=F32)
    g = dot(xa, wg_ref[:hcols]) + dot(xc, wg_ref[hcols:])
    u = dot(xa, wu_ref[:hcols]) + dot(xc, wu_ref[hcols:])
    o_ref[...] = _silu_mul(g, u).astype(o_ref.dtype)


def _shared_up(xw, wg, wu, tm):
    t, hw = xw.shape
    d, de = wg.shape
    wspec = pl.BlockSpec((d, de), lambda i: (0, 0))
    return pl.pallas_call(
        _shared_up_body, grid=(t // tm,),
        in_specs=[pl.BlockSpec((tm, hw), lambda i: (i, 0)), wspec, wspec],
        out_specs=pl.BlockSpec((tm, de), lambda i: (i, 0)),
        out_shape=jax.ShapeDtypeStruct((t, de), BF16),
        compiler_params=_cparams(("parallel",), 48), name="moe_shared_up")(xw, wg, wu)


def _combine_body(dcur_ref, dnext_ref, w_ref, x_ref, hs_ref, wd_ref, g_ref, b_ref, yb_ref,
                  o_ref, ob_ref, buf0_ref, buf1_ref, acc_ref, sem, *, nsteps, tb):
    i = pl.program_id(0)
    tt = x_ref.shape[0]
    hcols = x_ref.shape[1] // 2

    def row_copy(d_ref, buf, s, t, k):
        return pltpu.make_async_copy(yb_ref.at[pl.ds(d_ref[k, t], 1), :], buf.at[k, pl.ds(t, 1), :], sem.at[s])

    def drain(d_ref, buf, s):
        def body(t, c):
            for k in range(TOP_K):
                row_copy(d_ref, buf, s, t, k).wait()
            return c
        lax.fori_loop(0, tt, body, 0)

    @pl.when(i == 0)
    def _():
        def body(t, c):
            for k in range(TOP_K):
                row_copy(dcur_ref, buf0_ref, 0, t, k).start()
            return c
        lax.fori_loop(0, tt, body, 0)

    def step(cur, nxt, s):
        drain(dcur_ref, cur, s)
        acc_ref[...] = ALPHA * x_ref[...] + jnp.dot(hs_ref[...], wd_ref[...], preferred_element_type=F32)

        def body(j, c):
            r0 = pl.multiple_of(j * tb, tb)
            for off in range(tb):
                for k in range(TOP_K):
                    row_copy(dnext_ref, nxt, 1 - s, r0 + off, k).start()
            rows = pl.ds(r0, tb)
            a = acc_ref[rows, :]
            hi_acc, lo_acc = a[:, :hcols], a[:, hcols:]
            for k in range(TOP_K):
                gate = w_ref[rows, k:k + 1]
                hi, lo = _unpack_halves(cur[k, rows, :])
                hi_acc = hi_acc + gate * hi
                lo_acc = lo_acc + gate * lo
            o = _layernorm(jnp.concatenate([hi_acc, lo_acc], axis=1), g_ref[...], b_ref[...])
            o_ref[rows, :] = o
            ob_ref[rows, :] = o.astype(BF16)
            return c
        lax.fori_loop(0, tt // tb, body, 0)

        @pl.when(i == nsteps - 1)
        def _():
            drain(dnext_ref, nxt, 1 - s)

    @pl.when(i % 2 == 0)
    def _():
        step(buf0_ref, buf1_ref, 0)

    @pl.when(i % 2 == 1)
    def _():
        step(buf1_ref, buf0_ref, 1)


def _combine(dest, gates, x1, hs, wd, g, b, yb, tt):
    t, d = x1.shape
    de = hs.shape[1]
    nsteps = t // tt
    smem = functools.partial(pl.BlockSpec, memory_space=pltpu.SMEM)
    row = pl.BlockSpec((tt, d), lambda i: (i, 0))
    vec = pl.BlockSpec((1, d), lambda i: (0, 0))
    return pl.pallas_call(
        functools.partial(_combine_body, nsteps=nsteps, tb=32), grid=(nsteps,),
        in_specs=[smem((TOP_K, tt), lambda i: (0, i)),
                  smem((TOP_K, tt), lambda i: (0, jnp.minimum(i + 1, nsteps - 1))),
                  pl.BlockSpec((tt, TOP_K), lambda i: (i, 0)), row,
                  pl.BlockSpec((tt, de), lambda i: (i, 0)), pl.BlockSpec((de, d), lambda i: (0, 0)),
                  vec, vec, pl.BlockSpec(memory_space=pl.ANY)],
        out_specs=[row, row],
        out_shape=[jax.ShapeDtypeStruct((t, d), F32), jax.ShapeDtypeStruct((t, d), BF16)],
        scratch_shapes=[pltpu.VMEM((TOP_K, tt, d // 2), U32), pltpu.VMEM((TOP_K, tt, d // 2), U32),
                        pltpu.VMEM((tt, d), F32), pltpu.SemaphoreType.DMA((2,))],
        compiler_params=_cparams(("arbitrary",), 48), name="moe_combine_ln2")(
            dest, dest, gates, x1, hs, wd, g, b, yb)


def _moe_ln2(x1, xw, router_wt, router_bias, wg, wu, wd, layer, shg, shu, shd, g2, b2):
    t, d = x1.shape
    ne = router_wt.shape[0]
    rows = MOE_ROWS
    nb = -(-(t * TOP_K + ne * (rows - 1)) // rows)
    nbp = -(-nb // LANES) * LANES
    eidx, gates, rank, cnt, crow = _route(x1, router_wt, router_bias, tm=512)
    pstart, pend, bexp, nused = _meta(cnt, crow, rows, nbp)
    dest = _dest(eidx, rank, pstart, tt=2048)
    xb = _dispatch(dest, cnt.astype(I32), pend, xw, nb * rows, rows, tt=512)
    yb = _experts(bexp[0], nused[0, :1], xb, wg, wu, wd, layer, rows)
    hs = _shared_up(xw, shg, shu, tm=1024)
    return _combine(dest, gates.T, x1, hs, shd, g2, b2, yb, tt=128)


def kernel(x_prompt, x_sample, w_in, fnet_w, s5_a_re, s5_a_im, s5_log_dt, s5_b_re, s5_b_im, s5_c_re, s5_c_im,
           s5_d, glu_w, glu_b, gn_fnet, gn_s5, w_out, ln1_g, ln1_b, router_w, router_bias, exp_w_gate, exp_w_up,
           exp_w_down, sh_w_gate, sh_w_up, sh_w_down, ln2_g, ln2_b):
    bp, sp, d = x_prompt.shape
    bs, ss, _ = x_sample.shape
    depth = w_in.shape[0]
    heads, hd = fnet_w.shape[1], fnet_w.shape[2]
    fw = heads * hd
    n2 = 128
    seqs = [(0, bp, sp // n2, n2), (bp * sp, bs, ss // n2, n2)]
    seq_lens = [sp] * bp + [ss] * bs
    consts = {"chan": _chan_const(hd)}
    for (_, _, n1, n2_) in seqs:
        consts[(n1, n2_)] = _dft_consts(n1, n2_, hd)

    x = jnp.concatenate([x_prompt.reshape(bp * sp, d), x_sample.reshape(bs * ss, d)], axis=0)
    xb = x.astype(BF16)
    for l in range(depth):
        ops = _s5_prep(s5_a_re[l], s5_a_im[l], s5_log_dt[l], s5_b_re[l], s5_b_im[l], s5_c_re[l], s5_c_im[l], s5_d[l])
        h = _mm(xb, w_in[l].astype(BF16), tm=1024, tn=512, out_dtype=BF16, name="w_in")
        f = _fnet_branch(h, fnet_w[l].astype(BF16), seqs, consts, heads, hd)
        ys = _s5_branch(h, fw, ops, seq_lens)
        sm = _glu(ys, glu_w[l].astype(BF16), glu_b[l].reshape(1, -1), tm=1024, tn=1024)
        y1 = _mix_out(f, sm, gn_fnet[l].reshape(1, -1), gn_s5[l].reshape(1, -1), w_out[l].astype(BF16), x,
                      tm=1024, tn=512)
        x1, xw = _ln1(y1, ln1_g[l].reshape(1, -1), ln1_b[l].reshape(1, -1), tm=256)
        x, xb = _moe_ln2(x1, xw, router_w[l].T, router_bias[l].reshape(-1, 1), exp_w_gate, exp_w_up,
                         exp_w_down, l, sh_w_gate[l].astype(BF16), sh_w_up[l].astype(BF16),
                         sh_w_down[l].astype(BF16), ln2_g[l].reshape(1, -1), ln2_b[l].reshape(1, -1))
    return (x[:bp * sp].reshape(bp, sp, d), x[bp * sp:].reshape(bs, ss, d))
```
